```python
import jax, jax.numpy as jnp
from jax import lax
import numpy as np

D_MODEL = 1024
BATCH = 1
SEQ = 16384
DEPTH = 1
DEC_BATCH = 128
DEC_SEQ = 1
PAST_LEN = 16384
PAGE_SIZE = 128

N_HEADS = 8
QK_NOPE = 128
QK_ROPE = 64
V_DIM = 128
Q_RANK = 256
KV_RANK = 128
ATT_WIDTH = N_HEADS * V_DIM
SOFTMAX_SCALE = (QK_NOPE + QK_ROPE) ** -0.5
ROPE_BASE = 10000.0
Q_BLOCK = 128
NEG_INF = -1e30
POOL_WINDOWS = (2, 4, 8, 16)
N_POOL_GROUPS = 4
POOL_GROUP_DIM = 128
POOL_WIDTH = N_POOL_GROUPS * POOL_GROUP_DIM
POOL_HIST = 15
EPS = 1e-6
OFF_KV = Q_RANK
OFF_KR = OFF_KV + KV_RANK
OFF_ZA = OFF_KR + QK_ROPE
OFF_UP = OFF_ZA + ATT_WIDTH
OFF_ZP = OFF_UP + POOL_WIDTH
OFF_GA = OFF_ZP + POOL_WIDTH
OFF_GP = OFF_GA + D_MODEL
IN_COLS = OFF_GP + D_MODEL
SPLIT_POINTS = (OFF_KV, OFF_KR, OFF_ZA, OFF_UP, OFF_ZP, OFF_GA, OFF_GP)

kernel_name = "mla_pool_gated_hybrid_step"


def rmsnorm(x, g):
    xf = x.astype(jnp.float32)
    y = xf * lax.rsqrt(jnp.mean(xf * xf, axis=-1, keepdims=True) + EPS)
    return (y * g.astype(jnp.float32)).astype(x.dtype)


def rope(x, pos):
    half = QK_ROPE // 2
    inv = ROPE_BASE ** (-jnp.arange(half, dtype=jnp.float32) / half)
    ang = pos.astype(jnp.float32)[:, None] * inv[None, :]
    shape = (1, pos.shape[0]) + (1,) * (x.ndim - 3) + (half,)
    cos = jnp.cos(ang).reshape(shape)
    sin = jnp.sin(ang).reshape(shape)
    xf = x.astype(jnp.float32)
    x1, x2 = xf[..., :half], xf[..., half:]
    return jnp.concatenate([x1 * cos - x2 * sin, x2 * cos + x1 * sin], axis=-1).astype(x.dtype)


def mla_attend(q_abs, q_r, q_pos, ckv, kr, k_pos):
    s = (jnp.einsum('bqhr,bkr->bhqk', q_abs, ckv).astype(jnp.float32)
         + jnp.einsum('bqhp,bkp->bhqk', q_r, kr).astype(jnp.float32)) * SOFTMAX_SCALE
    mask = q_pos[:, None] >= k_pos[None, :]
    s = jnp.where(mask[None, None], s, NEG_INF)
    p = jax.nn.softmax(s, axis=-1).astype(ckv.dtype)
    return jnp.einsum('bhqk,bkr->bqhr', p, ckv)


def blocked_causal_mla(q_abs, q_r, pos, ckv, kr):
    b, L = q_abs.shape[0], q_abs.shape[1]
    nb = L // Q_BLOCK
    qa_b = q_abs.reshape(b, nb, Q_BLOCK, N_HEADS, KV_RANK).swapaxes(0, 1)
    qr_b = q_r.reshape(b, nb, Q_BLOCK, N_HEADS, QK_ROPE).swapaxes(0, 1)
    qp_b = pos.reshape(nb, Q_BLOCK)

    def one_block(args):
        qa, qr, qp = args
        return mla_attend(qa, qr, qp, ckv, kr, pos)

    o = lax.map(one_block, (qa_b, qr_b, qp_b))
    return o.swapaxes(0, 1).reshape(b, L, N_HEADS, KV_RANK)


def pool_mixer(u, hist, pos, w_pool, pool_scale):
    L = u.shape[1]
    u_ext = jnp.concatenate([hist, u], axis=1)
    cs = jnp.cumsum(u_ext.astype(jnp.float32), axis=1)
    cs = jnp.pad(cs, ((0, 0), (1, 0), (0, 0)))
    end = cs[:, POOL_HIST + 1:POOL_HIST + 1 + L]
    groups = []
    for g, w in enumerate(POOL_WINDOWS):
        lo, hi = g * POOL_GROUP_DIM, (g + 1) * POOL_GROUP_DIM
        start = cs[:, POOL_HIST + 1 - w:POOL_HIST + 1 - w + L, lo:hi]
        cnt = jnp.minimum(w, pos + 1).astype(jnp.float32)[None, :, None]
        groups.append((end[..., lo:hi] - start) / cnt)
    pooled = jnp.concatenate(groups, axis=-1) - u.astype(jnp.float32)
    pooled = pooled.astype(u.dtype).reshape(u.shape[0], L, N_POOL_GROUPS, POOL_GROUP_DIM)
    mixed = jnp.einsum('bsgc,gcd->bsgd', pooled, w_pool).reshape(u.shape[0], L, POOL_WIDTH)
    return mixed * pool_scale, u_ext[:, -POOL_HIST:]


def mixer_layer(x, pos0, pool_hist, past_ckv, past_kr,
                g_pre, w_in, g_q, w_uq, g_kv, w_uk, w_uv, w_o_att,
                w_pool, pool_scale, w_o_pool, w_out):
    b, L, _ = x.shape
    h = rmsnorm(x, g_pre)
    z = h @ w_in
    q_lat, kv_lat, k_r, z_att, u_pool, z_pool, g_att, g_pool = jnp.split(z, SPLIT_POINTS, axis=-1)
    pos = pos0 + jnp.arange(L, dtype=jnp.int32)
    q = (rmsnorm(q_lat, g_q) @ w_uq).reshape(b, L, N_HEADS, QK_NOPE + QK_ROPE)
    q_nope = q[..., :QK_NOPE]
    q_r = rope(q[..., QK_NOPE:], pos)
    q_abs = jnp.einsum('bshn,rhn->bshr', q_nope, w_uk)
    ckv_new = rmsnorm(kv_lat, g_kv)
    kr_new = rope(k_r, pos)
    if past_ckv is None:
        o_lat = blocked_causal_mla(q_abs, q_r, pos, ckv_new, kr_new)
    else:
        ckv_all = jnp.concatenate([past_ckv, ckv_new], axis=1)
        kr_all = jnp.concatenate([past_kr, kr_new], axis=1)
        k_pos = jnp.arange(ckv_all.shape[1], dtype=jnp.int32)
        o_lat = mla_attend(q_abs, q_r, pos, ckv_all, kr_all, k_pos)
    o = jnp.einsum('bshr,rhv->bshv', o_lat, w_uv).reshape(b, L, ATT_WIDTH)
    y_att = (o * jax.nn.silu(z_att)) @ w_o_att
    p, pool_state = pool_mixer(u_pool, pool_hist, pos, w_pool, pool_scale)
    y_pool = (p * jax.nn.silu(z_pool)) @ w_o_pool
    merged = jax.nn.sigmoid(g_att) * y_att + jax.nn.sigmoid(g_pool) * y_pool
    x = x + merged @ w_out
    return x, ckv_new, kr_new, pool_state


def setup_inputs(seed: int = 0) -> dict:
    key = jax.random.key(seed)
    ks = jax.random.split(key, 24)
    n_pages = PAST_LEN // PAGE_SIZE
    n_used = DEC_BATCH * n_pages
    n_phys = n_used + n_used // 4

    def nrm(k, shape, scale):
        return jax.random.normal(k, shape, dtype=jnp.float32) * scale

    def gain(k, shape):
        return 1.0 + 0.05 * jax.random.normal(k, shape, dtype=jnp.float32)

    page_table = jax.random.permutation(ks[0], n_phys)[:n_used].reshape(DEC_BATCH, n_pages).astype(jnp.int32)
    return {
        "x_prompt": nrm(ks[1], (BATCH, SEQ, D_MODEL), 1.0),
        "x_sample": nrm(ks[2], (DEC_BATCH, DEC_SEQ, D_MODEL), 1.0),
        "cache_ckv": nrm(ks[3], (DEPTH, n_phys, PAGE_SIZE, KV_RANK), 1.0),
        "cache_krope": nrm(ks[4], (DEPTH, n_phys, PAGE_SIZE, QK_ROPE), 1.0),
        "state_pool": nrm(ks[5], (DEPTH, DEC_BATCH, POOL_HIST, POOL_WIDTH), 1.0),
        "page_table": page_table,
        "g_pre": gain(ks[6], (DEPTH, D_MODEL)),
        "w_in": nrm(ks[7], (DEPTH, D_MODEL, IN_COLS), D_MODEL ** -0.5),
        "g_q": gain(ks[8], (DEPTH, Q_RANK)),
        "w_uq": nrm(ks[9], (DEPTH, Q_RANK, N_HEADS * (QK_NOPE + QK_ROPE)), Q_RANK ** -0.5),
        "g_kv": gain(ks[10], (DEPTH, KV_RANK)),
        "w_uk": nrm(ks[11], (DEPTH, KV_RANK, N_HEADS, QK_NOPE), KV_RANK ** -0.5),
        "w_uv": nrm(ks[12], (DEPTH, KV_RANK, N_HEADS, V_DIM), KV_RANK ** -0.5),
        "w_o_att": nrm(ks[13], (DEPTH, ATT_WIDTH, D_MODEL), ATT_WIDTH ** -0.5),
        "w_pool": nrm(ks[14], (DEPTH, N_POOL_GROUPS, POOL_GROUP_DIM, POOL_GROUP_DIM), POOL_GROUP_DIM ** -0.5),
        "pool_scale": gain(ks[15], (DEPTH, POOL_WIDTH)),
        "w_o_pool": nrm(ks[16], (DEPTH, POOL_WIDTH, D_MODEL), POOL_WIDTH ** -0.5),
        "w_out": nrm(ks[17], (DEPTH, D_MODEL, D_MODEL), D_MODEL ** -0.5),
        "g_final": gain(ks[18], (D_MODEL,)),
    }


def reference(x_prompt, x_sample, cache_ckv, cache_krope, state_pool, page_table,
              g_pre, w_in, g_q, w_uq, g_kv, w_uk, w_uv, w_o_att,
              w_pool, pool_scale, w_o_pool, w_out, g_final):
    dec_b = x_sample.shape[0]
    past_len = page_table.shape[1] * cache_ckv.shape[2]
    xp, xs = x_prompt, x_sample
    ckv_p, kr_p, pool_p, ckv_s, kr_s, pool_s = [], [], [], [], [], []
    for l in range(DEPTH):
        lw = (g_pre[l], w_in[l], g_q[l], w_uq[l], g_kv[l], w_uk[l], w_uv[l], w_o_att[l],
              w_pool[l], pool_scale[l], w_o_pool[l], w_out[l])
        hist0 = jnp.zeros((xp.shape[0], POOL_HIST, POOL_WIDTH), dtype=xp.dtype)
        xp, c_p, k_p, s_p = mixer_layer(xp, 0, hist0, None, None, *lw)
        past_ckv = cache_ckv[l][page_table].reshape(dec_b, past_len, KV_RANK)
        past_kr = cache_krope[l][page_table].reshape(dec_b, past_len, QK_ROPE)
        xs, c_s, k_s, s_s = mixer_layer(xs, past_len, state_pool[l], past_ckv, past_kr, *lw)
        ckv_p.append(c_p); kr_p.append(k_p); pool_p.append(s_p)
        ckv_s.append(c_s); kr_s.append(k_s); pool_s.append(s_s)
    y_prompt = rmsnorm(xp, g_final)
    y_sample = rmsnorm(xs, g_final)
    return (y_prompt, y_sample, jnp.stack(ckv_p), jnp.stack(kr_p), jnp.stack(pool_p),
            jnp.stack(ckv_s), jnp.stack(kr_s), jnp.stack(pool_s))
```

```python
import functools

import jax
import jax.numpy as jnp
from jax import lax
from jax.experimental import pallas as pl
from jax.experimental.pallas import tpu as pltpu

F32 = jnp.float32
BF16 = jnp.bfloat16

N_HEADS = 8
QK_NOPE = 128
QK_ROPE = 64
V_DIM = 128
Q_RANK = 256
KV_RANK = 128
ROPE_BASE = 10000.0
POOL_WINDOWS = (2, 4, 8, 16)
POOL_GROUP_DIM = 128
POOL_WIDTH = len(POOL_WINDOWS) * POOL_GROUP_DIM
POOL_HIST = 15
EPS = 1e-6
NEG_INF = -1e30
SOFTMAX_SCALE = (QK_NOPE + QK_ROPE) ** -0.5

LANES = 128
KEYS_PER_ROW = LANES // N_HEADS
QCAT = 2 * LANES
VMEM_LIMIT = 56 * 1024 * 1024


def _dot(a, b):
    return jnp.dot(a, b, preferred_element_type=F32)


def _dot_nt(a, b):
    return lax.dot_general(a, b, (((1,), (1,)), ((), ())), preferred_element_type=F32)


def _rms(x, g):
    return x * lax.rsqrt(jnp.mean(x * x, axis=-1, keepdims=True) + EPS) * g


def _silu(x):
    return x * jax.nn.sigmoid(x)


def _rope_pair(t):
    return t + pltpu.roll(t, QK_ROPE, axis=1)


def _project(x, cs, g_pre, w_a, g_q, w_qn, w_qr, w_ukt, g_kv):
    h = _rms(x, g_pre).astype(BF16)
    za = _dot(h, w_a)
    qn = _rms(za[:, :Q_RANK], g_q).astype(BF16)
    ckv = _rms(za[:, Q_RANK:Q_RANK + KV_RANK], g_kv)
    kr2 = _rope_pair(za[:, Q_RANK + KV_RANK:] * cs)
    q_nope = _dot(qn, w_qn)
    q_rope = _dot(qn, w_qr)
    qa, qr = [], []
    for hd in range(N_HEADS):
        sl = slice(hd * LANES, (hd + 1) * LANES)
        qa.append(_dot(q_nope[:, sl].astype(BF16), w_ukt[hd]) * SOFTMAX_SCALE)
        qr.append(_rope_pair(q_rope[:, sl] * cs) * SOFTMAX_SCALE)
    return h, qa, qr, ckv, kr2


def _pool_branch(h, win_sum, u, rcnt, w_zp, w_gp, w_pool, pool_scale, w_o_pool):
    mixed = []
    for g in range(len(POOL_WINDOWS)):
        sl = slice(g * POOL_GROUP_DIM, (g + 1) * POOL_GROUP_DIM)
        pooled = win_sum[g] * rcnt[g] - u[:, sl]
        mixed.append(_dot(pooled.astype(BF16), w_pool[g]))
    p = jnp.concatenate(mixed, axis=1) * pool_scale
    z_pool = _dot(h, w_zp)
    y_pool = _dot((p * _silu(z_pool)).astype(BF16), w_o_pool)
    return jax.nn.sigmoid(_dot(h, w_gp)) * y_pool


def _prompt_proj_kernel(x_ref, cs_ref, hist_ref, g_pre_ref, w_a_ref, w_za_ref, w_up_ref,
                        w_zp_ref, w_ga_ref, w_gp_ref, g_q_ref, w_qn_ref, w_qr_ref, w_ukt_ref,
                        g_kv_ref, w_pool_ref, pscale_ref, w_opool_ref,
                        qcat_ref, kcat_ref, vt_ref, ckv_ref, kr_ref, sz_ref, ga_ref, yp_ref,
                        pstate_ref, uext_ref, *, tm):
    i = pl.program_id(0)
    hist_rows = uext_ref.shape[0] - tm

    @pl.when(i == 0)
    def _():
        uext_ref[0:hist_rows, :] = hist_ref[...]

    cs = cs_ref[...]
    h, qa, qr, ckv, kr2 = _project(x_ref[...], cs, g_pre_ref[...], w_a_ref[...], g_q_ref[...],
                                   w_qn_ref[...], w_qr_ref[...], w_ukt_ref, g_kv_ref[...])
    for hd in range(N_HEADS):
        qcat_ref[hd, :, 0:LANES] = qa[hd].astype(BF16)
        qcat_ref[hd, :, LANES:QCAT] = qr[hd].astype(BF16)
    lane = lax.broadcasted_iota(jnp.int32, kr2.shape, 1)
    kcat_ref[:, 0:LANES] = ckv.astype(BF16)
    kcat_ref[:, LANES:QCAT] = jnp.where(lane < QK_ROPE, kr2, 0.0).astype(BF16)
    vt_ref[0] = ckv.T.astype(BF16)
    ckv_ref[...] = ckv
    kr_ref[...] = kr2[:, :QK_ROPE]

    sz_ref[...] = _silu(_dot(h, w_za_ref[...]))
    ga_ref[...] = jax.nn.sigmoid(_dot(h, w_ga_ref[...]))

    u = _dot(h, w_up_ref[...])
    uext_ref[hist_rows:, :] = u
    pos = i * tm + lax.broadcasted_iota(jnp.int32, (tm, 1), 0)
    win_sum, rcnt = [], []
    for g, w in enumerate(POOL_WINDOWS):
        c0 = g * POOL_GROUP_DIM
        acc = u[:, c0:c0 + POOL_GROUP_DIM]
        for s in range(1, w):
            acc = acc + uext_ref[hist_rows - s:hist_rows - s + tm, c0:c0 + POOL_GROUP_DIM]
        win_sum.append(acc)
        rcnt.append(1.0 / jnp.minimum(w, pos + 1).astype(F32))
    yp_ref[...] = _pool_branch(h, win_sum, u, rcnt, w_zp_ref[...], w_gp_ref[...], w_pool_ref,
                               pscale_ref[...], w_opool_ref[...])
    tail = uext_ref[tm:tm + hist_rows, :]
    pstate_ref[...] = tail
    uext_ref[0:hist_rows, :] = tail


def _prompt_attn_kernel(q_ref, k_ref, vt_ref, o_ref, m_sc, l_sc, acc_sc, *, t):
    i = pl.program_id(0)
    m_sc[...] = jnp.full(m_sc.shape, NEG_INF, F32)
    l_sc[...] = jnp.zeros(l_sc.shape, F32)
    acc_sc[...] = jnp.zeros(acc_sc.shape, F32)

    def chunk(j, masked):
        kc = k_ref[pl.ds(pl.multiple_of(j * t, t), t), :]
        vt = vt_ref[j]
        if masked:
            key = lax.broadcasted_iota(jnp.int32, (t, t), 0)
            qry = lax.broadcasted_iota(jnp.int32, (t, t), 1)
            keep = key <= qry
        for hd in range(N_HEADS):
            st = _dot_nt(kc, q_ref[hd])
            if masked:
                st = jnp.where(keep, st, NEG_INF)
            m_prev = m_sc[hd]
            m_new = jnp.maximum(m_prev, jnp.max(st, axis=0, keepdims=True))
            alpha = jnp.exp(m_prev - m_new)
            p = jnp.exp(st - m_new)
            l_sc[hd] = alpha * l_sc[hd] + jnp.sum(p, axis=0, keepdims=True)
            acc_sc[hd] = alpha * acc_sc[hd] + _dot(vt, p.astype(BF16))
            m_sc[hd] = m_new

    def body(j, carry):
        chunk(j, False)
        return carry

    lax.fori_loop(0, i, body, 0)
    chunk(i, True)
    for hd in range(N_HEADS):
        o = acc_sc[hd] / l_sc[hd]
        o_ref[:, hd * KV_RANK:(hd + 1) * KV_RANK] = o.T.astype(BF16)


def _epilogue_kernel(o_ref, sz_ref, ga_ref, yp_ref, x_ref, w_uv_ref, w_oatt_ref, w_out_ref,
                     g_final_ref, y_ref):
    o = o_ref[...]
    o2 = jnp.concatenate(
        [_dot(o[:, hd * KV_RANK:(hd + 1) * KV_RANK], w_uv_ref[hd]) for hd in range(N_HEADS)], axis=1)
    y_att = _dot((o2 * sz_ref[...]).astype(BF16), w_oatt_ref[...])
    merged = ga_ref[...] * y_att + yp_ref[...]
    xo = x_ref[...] + _dot(merged.astype(BF16), w_out_ref[...])
    y_ref[...] = _rms(xo, g_final_ref[...])


def _sample_proj_kernel(x_ref, cs_ref, hist_ref, g_pre_ref, w_a_ref, w_za_ref, w_up_ref,
                        w_zp_ref, w_ga_ref, w_gp_ref, g_q_ref, w_qn_ref, w_qr_ref, w_ukt_ref,
                        g_kv_ref, w_pool_ref, pscale_ref, w_opool_ref,
                        qa_ref, qr_ref, ckv_ref, kr_ref, sz_ref, ga_ref, yp_ref, pstate_ref,
                        *, past_len):
    cs = cs_ref[...]
    h, qa, qr, ckv, kr2 = _project(x_ref[...], cs, g_pre_ref[...], w_a_ref[...], g_q_ref[...],
                                   w_qn_ref[...], w_qr_ref[...], w_ukt_ref, g_kv_ref[...])
    for hd in range(N_HEADS):
        qa_ref[:, hd * LANES:(hd + 1) * LANES] = qa[hd]
        qr_ref[:, hd * LANES:(hd + 1) * LANES] = qr[hd]
    ckv_ref[...] = ckv
    kr_ref[...] = kr2[:, :QK_ROPE]
    sz_ref[...] = _silu(_dot(h, w_za_ref[...]))
    ga_ref[...] = jax.nn.sigmoid(_dot(h, w_ga_ref[...]))

    u = _dot(h, w_up_ref[...])
    win_sum, rcnt = [], []
    for g, w in enumerate(POOL_WINDOWS):
        c0 = g * POOL_GROUP_DIM
        acc = u[:, c0:c0 + POOL_GROUP_DIM]
        for s in range(1, w):
            off = (POOL_HIST - s) * POOL_WIDTH + c0
            acc = acc + hist_ref[:, off:off + POOL_GROUP_DIM]
        win_sum.append(acc)
        rcnt.append(1.0 / float(min(w, past_len + 1)))
    yp_ref[...] = _pool_branch(h, win_sum, u, rcnt, w_zp_ref[...], w_gp_ref[...], w_pool_ref,
                               pscale_ref[...], w_opool_ref[...])
    keep = (POOL_HIST - 1) * POOL_WIDTH
    pstate_ref[:, 0:keep] = hist_ref[:, POOL_WIDTH:]
    pstate_ref[:, keep:] = u


def _lane_group_reduce(v, op):
    sh = N_HEADS
    while sh < LANES:
        v = op(v, pltpu.roll(v, sh, axis=1))
        sh *= 2
    return v


def _diag_to_col(v):
    sub = lax.broadcasted_iota(jnp.int32, v.shape, 0)
    lane = lax.broadcasted_iota(jnp.int32, v.shape, 1)
    return jnp.sum(jnp.where(sub == lane, v, 0.0), axis=1, keepdims=True)


def _latent_rows(kbuf, slot):
    n_pairs, _, page, width = kbuf.shape[1:]
    halves = []
    for half in range(2):
        cols = []
        for jj in range(KEYS_PER_ROW // 2):
            r0 = (half * (KEYS_PER_ROW // 2) + jj) * 8
            for par in range(2):
                cols.append(kbuf[slot, :, par, r0:r0 + 8, :].reshape(n_pairs * 8, width))
        halves.append(jnp.concatenate(cols, axis=1))
    return jnp.concatenate(halves, axis=0).astype(BF16)


def _rotary_rows(rbuf, slot):
    n_pairs = rbuf.shape[1]
    xt = jnp.swapaxes(rbuf[slot], 1, 2)
    halves = []
    for half in range(2):
        cols = []
        for jj in range(KEYS_PER_ROW // 2):
            r0 = (half * (KEYS_PER_ROW // 2) + jj) * 8
            cols.append(xt[:, r0:r0 + 8, :].reshape(n_pairs * 8, LANES))
        halves.append(jnp.concatenate(cols, axis=1))
    return jnp.concatenate(halves, axis=0).astype(BF16)


def _sample_attn_kernel(pt_ref, qat_ref, qrt_ref, knew_ref, rnew_ref, ck_hbm, krt_hbm, o_ref,
                        kbuf, rbuf, sems, *, n_pages):
    b = pl.program_id(0)
    nb = pl.num_programs(0)
    slot = b % 2

    def copies(bb, sl, q):
        out = []
        for par in range(2):
            pid = pt_ref[bb * n_pages + 2 * q + par]
            out.append(pltpu.make_async_copy(ck_hbm.at[pid], kbuf.at[sl, q, par], sems.at[0, sl]))
            out.append(pltpu.make_async_copy(
                krt_hbm.at[pid], rbuf.at[sl, q, par * QK_ROPE:(par + 1) * QK_ROPE, :],
                sems.at[1, sl]))
        return out

    def start_all(bb, sl):
        def body(q, c):
            for cp in copies(bb, sl, q):
                cp.start()
            return c
        lax.fori_loop(0, n_pages // 2, body, 0)

    def wait_all(bb, sl):
        def body(q, c):
            for cp in copies(bb, sl, q):
                cp.wait()
            return c
        lax.fori_loop(0, n_pages // 2, body, 0)

    @pl.when(b == 0)
    def _():
        start_all(b, slot)

    @pl.when(b + 1 < nb)
    def _():
        start_all(b + 1, 1 - slot)

    wait_all(b, slot)

    qat = qat_ref[0]
    qrt = qrt_ref[0]
    lane_a = lax.broadcasted_iota(jnp.int32, qat.shape, 1) // N_HEADS
    lane_r = lax.broadcasted_iota(jnp.int32, qrt.shape, 1) // N_HEADS
    zero = jnp.zeros((), BF16)
    wk = jnp.concatenate([jnp.where(lane_a == j, qat, zero) for j in range(KEYS_PER_ROW)], axis=0)
    wr = jnp.concatenate([jnp.where(lane_r == j, qrt, zero) for j in range(KEYS_PER_ROW)], axis=0)

    kb = _latent_rows(kbuf, slot)
    rb = _rotary_rows(rbuf, slot)
    s2 = _dot(kb, wk) + _dot(rb, wr)

    knew = knew_ref[0].astype(BF16)
    rnew = rnew_ref[0].astype(BF16)
    s_new = (_dot(jnp.broadcast_to(knew, (8, KV_RANK)), qat)
             + _dot(jnp.broadcast_to(rnew, (8, QK_ROPE)), qrt))

    m = jnp.max(s2, axis=0, keepdims=True)
    m = _lane_group_reduce(jnp.broadcast_to(m, (8, LANES)), jnp.maximum)
    m = jnp.maximum(m, s_new)
    p2 = jnp.exp(s2 - m[0:1])
    p_new = jnp.exp(s_new - m)
    l_lane = _lane_group_reduce(
        jnp.broadcast_to(jnp.sum(p2, axis=0, keepdims=True), (8, LANES)), jnp.add)
    p_new_col = _diag_to_col(p_new)
    l_col = _diag_to_col(l_lane) + p_new_col

    g = _dot(p2.T.astype(BF16), kb)
    o = p_new_col.astype(BF16).astype(F32) * knew.astype(F32)
    for j in range(KEYS_PER_ROW):
        o = o + g[j * N_HEADS:(j + 1) * N_HEADS, j * KV_RANK:(j + 1) * KV_RANK]
    o_ref[0] = o / l_col


def _const_spec(shape):
    zeros = (0,) * len(shape)
    return pl.BlockSpec(shape, lambda *_: zeros)


def _prep_weights(g_pre, w_in, g_q, w_uq, g_kv, w_uk, w_uv, w_o_att, w_pool, pool_scale,
                  w_o_pool, w_out, g_final):
    d_model = w_in.shape[0]
    att_w = N_HEADS * V_DIM
    half = QK_ROPE // 2
    o_kv = Q_RANK
    o_kr = o_kv + KV_RANK
    o_za = o_kr + QK_ROPE
    o_up = o_za + att_w
    o_zp = o_up + POOL_WIDTH
    o_ga = o_zp + POOL_WIDTH
    o_gp = o_ga + d_model
    w_kr = w_in[:, o_kr:o_za]
    w_a = jnp.concatenate([w_in[:, :o_kr], w_kr, -w_kr[:, half:], w_kr[:, :half]], axis=1)
    uq = w_uq.reshape(Q_RANK, N_HEADS, QK_NOPE + QK_ROPE)
    w_qn = uq[:, :, :QK_NOPE].reshape(Q_RANK, N_HEADS * QK_NOPE)
    x1, x2 = uq[:, :, QK_NOPE:QK_NOPE + half], uq[:, :, QK_NOPE + half:]
    w_qr = jnp.concatenate([x1, x2, -x2, x1], axis=2).reshape(Q_RANK, N_HEADS * LANES)
    bf = lambda a: a.astype(BF16)
    row = lambda a: a.reshape(1, -1).astype(F32)
    return dict(
        g_pre=row(g_pre), w_a=bf(w_a), w_za=bf(w_in[:, o_za:o_up]), w_up=bf(w_in[:, o_up:o_zp]),
        w_zp=bf(w_in[:, o_zp:o_ga]), w_ga=bf(w_in[:, o_ga:o_gp]), w_gp=bf(w_in[:, o_gp:]),
        g_q=row(g_q), w_qn=bf(w_qn), w_qr=bf(w_qr),
        w_ukt=bf(jnp.transpose(w_uk, (1, 2, 0))),
        g_kv=row(g_kv), w_pool=bf(w_pool), pscale=row(pool_scale), w_opool=bf(w_o_pool),
        w_uv=bf(jnp.transpose(w_uv, (1, 0, 2))),
        w_oatt=bf(w_o_att), w_out=bf(w_out), g_final=row(g_final))


_PROJ_WEIGHTS = ("g_pre", "w_a", "w_za", "w_up", "w_zp", "w_ga", "w_gp", "g_q", "w_qn", "w_qr",
                 "w_ukt", "g_kv", "w_pool", "pscale", "w_opool")


def _rope_table(pos):
    half = QK_ROPE // 2
    inv = ROPE_BASE ** (-jnp.arange(half, dtype=F32) / half)
    ang = pos.astype(F32)[:, None] * inv[None, :]
    c, s = jnp.cos(ang), jnp.sin(ang)
    return jnp.concatenate([c, c, s, s], axis=1)


def _prompt_proj(x, cs, hist, wts, tm):
    seq, d_model = x.shape
    n = seq // tm
    hist_rows = hist.shape[0]
    ws = [wts[k] for k in _PROJ_WEIGHTS]
    rows = lambda width: pl.BlockSpec((tm, width), lambda i: (i, 0))
    out_shape = (
        jax.ShapeDtypeStruct((N_HEADS, seq, QCAT), BF16),
        jax.ShapeDtypeStruct((seq, QCAT), BF16),
        jax.ShapeDtypeStruct((n, KV_RANK, tm), BF16),
        jax.ShapeDtypeStruct((seq, KV_RANK), F32),
        jax.ShapeDtypeStruct((seq, QK_ROPE), F32),
        jax.ShapeDtypeStruct((seq, d_model), F32),
        jax.ShapeDtypeStruct((seq, d_model), F32),
        jax.ShapeDtypeStruct((seq, d_model), F32),
        jax.ShapeDtypeStruct((hist_rows, POOL_WIDTH), F32),
    )
    out_specs = (
        pl.BlockSpec((N_HEADS, tm, QCAT), lambda i: (0, i, 0)),
        rows(QCAT),
        pl.BlockSpec((1, KV_RANK, tm), lambda i: (i, 0, 0)),
        rows(KV_RANK), rows(QK_ROPE), rows(d_model), rows(d_model), rows(d_model),
        _const_spec((hist_rows, POOL_WIDTH)),
    )
    return pl.pallas_call(
        functools.partial(_prompt_proj_kernel, tm=tm),
        grid=(n,),
        in_specs=[rows(d_model), rows(LANES), _const_spec(hist.shape)]
        + [_const_spec(w.shape) for w in ws],
        out_specs=out_specs,
        out_shape=out_shape,
        scratch_shapes=[pltpu.VMEM((hist_rows + tm, POOL_WIDTH), F32)],
        compiler_params=pltpu.CompilerParams(
            dimension_semantics=("arbitrary",), vmem_limit_bytes=VMEM_LIMIT),
    )(x, cs, hist, *ws)


def _prompt_attn(qcat, kcat, vt, t):
    seq = kcat.shape[0]
    n = seq // t
    return pl.pallas_call(
        functools.partial(_prompt_attn_kernel, t=t),
        grid=(n,),
        in_specs=[pl.BlockSpec((N_HEADS, t, QCAT), lambda i: (0, i, 0)),
                  pl.BlockSpec(memory_space=pltpu.VMEM),
                  pl.BlockSpec(memory_space=pltpu.VMEM)],
        out_specs=pl.BlockSpec((t, N_HEADS * KV_RANK), lambda i: (i, 0)),
        out_shape=jax.ShapeDtypeStruct((seq, N_HEADS * KV_RANK), BF16),
        scratch_shapes=[pltpu.VMEM((N_HEADS, 1, t), F32),
                        pltpu.VMEM((N_HEADS, 1, t), F32),
                        pltpu.VMEM((N_HEADS, KV_RANK, t), F32)],
        compiler_params=pltpu.CompilerParams(
            dimension_semantics=("arbitrary",), vmem_limit_bytes=VMEM_LIMIT),
    )(qcat, kcat, vt)


def _epilogue(o, sz, ga, yp, x, wts, tm):
    n_rows, d_model = x.shape
    rows = pl.BlockSpec((tm, d_model), lambda i: (i, 0))
    ws = [wts[k] for k in ("w_uv", "w_oatt", "w_out", "g_final")]
    return pl.pallas_call(
        _epilogue_kernel,
        grid=(n_rows // tm,),
        in_specs=[rows] * 5 + [_const_spec(w.shape) for w in ws],
        out_specs=rows,
        out_shape=jax.ShapeDtypeStruct((n_rows, d_model), F32),
        compiler_params=pltpu.CompilerParams(
            dimension_semantics=("arbitrary",), vmem_limit_bytes=VMEM_LIMIT),
    )(o, sz, ga, yp, x, *ws)


def _sample_proj(x, cs, hist2d, wts, past_len):
    b, d_model = x.shape
    ws = [wts[k] for k in _PROJ_WEIGHTS]
    wide = jax.ShapeDtypeStruct((b, d_model), F32)
    out_shape = (
        jax.ShapeDtypeStruct((b, N_HEADS * LANES), F32),
        jax.ShapeDtypeStruct((b, N_HEADS * LANES), F32),
        jax.ShapeDtypeStruct((b, KV_RANK), F32),
        jax.ShapeDtypeStruct((b, QK_ROPE), F32),
        wide, wide, wide,
        jax.ShapeDtypeStruct(hist2d.shape, F32),
    )
    args = (x, cs, hist2d, *ws)
    return pl.pallas_call(
        functools.partial(_sample_proj_kernel, past_len=past_len),
        grid=(1,),
        in_specs=[_const_spec(a.shape) for a in args],
        out_specs=tuple(_const_spec(s.shape) for s in out_shape),
        out_shape=out_shape,
        compiler_params=pltpu.CompilerParams(
            dimension_semantics=("arbitrary",), vmem_limit_bytes=VMEM_LIMIT),
    )(*args)


def _sample_attn(page_table, qat, qrt, knew, rnew, cache_ckv, cache_krope):
    b, n_pages = page_table.shape
    page = cache_ckv.shape[1]
    per_batch = lambda rows, width: pl.BlockSpec((1, rows, width), lambda i, pt: (i, 0, 0))
    grid_spec = pltpu.PrefetchScalarGridSpec(
        num_scalar_prefetch=1,
        grid=(b,),
        in_specs=[per_batch(KV_RANK, LANES), per_batch(QK_ROPE, LANES),
                  per_batch(1, KV_RANK), per_batch(1, QK_ROPE),
                  pl.BlockSpec(memory_space=pl.ANY),
                  pl.BlockSpec(memory_space=pl.ANY)],
        out_specs=per_batch(N_HEADS, KV_RANK),
        scratch_shapes=[pltpu.VMEM((2, n_pages // 2, 2, page, KV_RANK), F32),
                        pltpu.VMEM((2, n_pages // 2, 2 * QK_ROPE, page), F32),
                        pltpu.SemaphoreType.DMA((2, 2))],
    )
    return pl.pallas_call(
        functools.partial(_sample_attn_kernel, n_pages=n_pages),
        grid_spec=grid_spec,
        out_shape=jax.ShapeDtypeStruct((b, N_HEADS, KV_RANK), F32),
        compiler_params=pltpu.CompilerParams(
            dimension_semantics=("arbitrary",), vmem_limit_bytes=VMEM_LIMIT),
    )(page_table.reshape(-1), qat, qrt, knew, rnew, cache_ckv, jnp.swapaxes(cache_krope, 1, 2))


def kernel(x_prompt, x_sample, cache_ckv, cache_krope, state_pool, page_table, g_pre, w_in, g_q,
           w_uq, g_kv, w_uk, w_uv, w_o_att, w_pool, pool_scale, w_o_pool, w_out, g_final):
    depth = g_pre.shape[0]
    batch, seq, d_model = x_prompt.shape
    dec_b, dec_seq, _ = x_sample.shape
    assert depth == 1 and batch == 1 and dec_seq == 1
    past_len = page_table.shape[1] * cache_ckv.shape[2]
    wts = _prep_weights(g_pre[0], w_in[0], g_q[0], w_uq[0], g_kv[0], w_uk[0], w_uv[0], w_o_att[0],
                        w_pool[0], pool_scale[0], w_o_pool[0], w_out[0], g_final)

    tile = 512
    hist_rows = 16
    xp = x_prompt.reshape(seq, d_model)
    cs_p = _rope_table(jnp.arange(seq, dtype=jnp.int32))
    hist0 = jnp.zeros((hist_rows, POOL_WIDTH), F32)
    qcat, kcat, vt, ckv_p, kr_p, sz, ga, yp, pstate = _prompt_proj(xp, cs_p, hist0, wts, tile)
    o_p = _prompt_attn(qcat, kcat, vt, tile)
    y_p = _epilogue(o_p, sz, ga, yp, xp, wts, tile)

    xs = x_sample.reshape(dec_b, d_model)
    cs_s = _rope_table(jnp.full((dec_b,), past_len, jnp.int32))
    hist2d = state_pool[0].reshape(dec_b, POOL_HIST * POOL_WIDTH)
    qa, qr, ckv_s, kr_s, sz_s, ga_s, yp_s, pstate_s = _sample_proj(xs, cs_s, hist2d, wts, past_len)
    qa3 = qa.reshape(dec_b, N_HEADS, LANES)
    qr3 = qr.reshape(dec_b, N_HEADS, LANES)[:, :, :QK_ROPE]
    qat = jnp.tile(jnp.transpose(qa3, (0, 2, 1)), (1, 1, KEYS_PER_ROW)).astype(BF16)
    qrt = jnp.tile(jnp.transpose(qr3, (0, 2, 1)), (1, 1, KEYS_PER_ROW)).astype(BF16)
    o_s = _sample_attn(page_table, qat, qrt, ckv_s.reshape(dec_b, 1, KV_RANK),
                       kr_s.reshape(dec_b, 1, QK_ROPE), cache_ckv[0], cache_krope[0])
    o_s = o_s.reshape(dec_b, N_HEADS * KV_RANK).astype(BF16)
    y_s = _epilogue(o_s, sz_s, ga_s, yp_s, xs, wts, dec_b)

    return (y_p.reshape(batch, seq, d_model),
            y_s.reshape(dec_b, 1, d_model),
            ckv_p.reshape(1, batch, seq, KV_RANK),
            kr_p.reshape(1, batch, seq, QK_ROPE),
            pstate[hist_rows - POOL_HIST:].reshape(1, batch, POOL_HIST, POOL_WIDTH),
            ckv_s.reshape(1, dec_b, 1, KV_RANK),
            kr_s.reshape(1, dec_b, 1, QK_ROPE),
            pstate_s.reshape(1, dec_b, POOL_HIST, POOL_WIDTH))
```

```python
import functools

import jax
import jax.numpy as jnp
from jax import lax
from jax.experimental import pallas as pl
from jax.experimental.pallas import tpu as pltpu

F32 = jnp.float32
BF16 = jnp.bfloat16

N_HEADS = 8
QK_NOPE = 128
QK_ROPE = 64
V_DIM = 128
Q_RANK = 256
KV_RANK = 128
ROPE_BASE = 10000.0
POOL_WINDOWS = (2, 4, 8, 16)
POOL_GROUP_DIM = 128
POOL_WIDTH = len(POOL_WINDOWS) * POOL_GROUP_DIM
POOL_HIST = 15
EPS = 1e-6
NEG_INF = -1e30
SOFTMAX_SCALE = (QK_NOPE + QK_ROPE) ** -0.5
LOG2_E = 1.4426950408889634
Q_SCALE = SOFTMAX_SCALE * LOG2_E

LANES = 128
KEYS_PER_ROW = LANES // N_HEADS
QCAT = 2 * LANES
VMEM_LIMIT = 56 * 1024 * 1024


def _dot(a, b):
    return jnp.dot(a, b, preferred_element_type=F32)


def _dot_nt(a, b):
    return lax.dot_general(a, b, (((1,), (1,)), ((), ())), preferred_element_type=F32)


def _rms(x, g):
    return x * lax.rsqrt(jnp.mean(x * x, axis=-1, keepdims=True) + EPS) * g


def _silu(x):
    return x * jax.nn.sigmoid(x)


def _rope_pair(t):
    return t + pltpu.roll(t, QK_ROPE, axis=1)


def _project(x, cs, g_pre, w_a, g_q, w_qn, w_qr, w_ukt, g_kv):
    h = _rms(x, g_pre).astype(BF16)
    za = _dot(h, w_a)
    qn = _rms(za[:, :Q_RANK], g_q).astype(BF16)
    ckv = _rms(za[:, Q_RANK:Q_RANK + KV_RANK], g_kv)
    kr2 = _rope_pair(za[:, Q_RANK + KV_RANK:] * cs)
    q_nope = _dot(qn, w_qn)
    q_rope = _dot(qn, w_qr)
    qa, qr = [], []
    for hd in range(N_HEADS):
        sl = slice(hd * LANES, (hd + 1) * LANES)
        qa.append(_dot(q_nope[:, sl].astype(BF16), w_ukt[hd]) * Q_SCALE)
        qr.append(_rope_pair(q_rope[:, sl] * cs) * Q_SCALE)
    return h, qa, qr, ckv, kr2


def _pool_branch(h, win_sum, u, rcnt, w_zp, w_gp, w_pool, pool_scale, w_o_pool):
    mixed = []
    for g in range(len(POOL_WINDOWS)):
        sl = slice(g * POOL_GROUP_DIM, (g + 1) * POOL_GROUP_DIM)
        pooled = win_sum[g] * rcnt[g] - u[:, sl]
        mixed.append(_dot(pooled.astype(BF16), w_pool[g]))
    p = jnp.concatenate(mixed, axis=1) * pool_scale
    z_pool = _dot(h, w_zp)
    y_pool = _dot((p * _silu(z_pool)).astype(BF16), w_o_pool)
    return jax.nn.sigmoid(_dot(h, w_gp)) * y_pool


def _prompt_proj_kernel(x_ref, cs_ref, hist_ref, g_pre_ref, w_a_ref, w_za_ref, w_up_ref,
                        w_zp_ref, w_ga_ref, w_gp_ref, g_q_ref, w_qn_ref, w_qr_ref, w_ukt_ref,
                        g_kv_ref, w_pool_ref, pscale_ref, w_opool_ref,
                        qcat_ref, kcat_ref, vt_ref, ckv_ref, kr_ref, sz_ref, ga_ref, yp_ref,
                        pstate_ref, uext_ref, *, tm):
    i = pl.program_id(0)
    hist_rows = uext_ref.shape[0] - tm

    @pl.when(i == 0)
    def _():
        uext_ref[0:hist_rows, :] = hist_ref[...]

    cs = cs_ref[...]
    h, qa, qr, ckv, kr2 = _project(x_ref[...], cs, g_pre_ref[...], w_a_ref[...], g_q_ref[...],
                                   w_qn_ref[...], w_qr_ref[...], w_ukt_ref, g_kv_ref[...])
    for hd in range(N_HEADS):
        qcat_ref[hd, :, 0:LANES] = qa[hd].astype(BF16)
        qcat_ref[hd, :, LANES:QCAT] = qr[hd].astype(BF16)
    lane = lax.broadcasted_iota(jnp.int32, kr2.shape, 1)
    kcat_ref[:, 0:LANES] = ckv.astype(BF16)
    kcat_ref[:, LANES:QCAT] = jnp.where(lane < QK_ROPE, kr2, 0.0).astype(BF16)
    vt_ref[0] = ckv.T.astype(BF16)
    ckv_ref[...] = ckv
    kr_ref[...] = kr2[:, :QK_ROPE]

    sz_ref[...] = _silu(_dot(h, w_za_ref[...]))
    ga_ref[...] = jax.nn.sigmoid(_dot(h, w_ga_ref[...]))

    u = _dot(h, w_up_ref[...])
    uext_ref[hist_rows:, :] = u
    pos = i * tm + lax.broadcasted_iota(jnp.int32, (tm, 1), 0)
    win_sum, rcnt = [], []
    for g, w in enumerate(POOL_WINDOWS):
        c0 = g * POOL_GROUP_DIM
        acc = u[:, c0:c0 + POOL_GROUP_DIM]
        for s in range(1, w):
            acc = acc + uext_ref[hist_rows - s:hist_rows - s + tm, c0:c0 + POOL_GROUP_DIM]
        win_sum.append(acc)
        rcnt.append(1.0 / jnp.minimum(w, pos + 1).astype(F32))
    yp_ref[...] = _pool_branch(h, win_sum, u, rcnt, w_zp_ref[...], w_gp_ref[...], w_pool_ref,
                               pscale_ref[...], w_opool_ref[...])
    tail = uext_ref[tm:tm + hist_rows, :]
    pstate_ref[...] = tail
    uext_ref[0:hist_rows, :] = tail


def _prompt_attn_kernel(q_ref, k_ref, vt_ref, o_ref, m_sc, l_sc, acc_sc, *, t):
    i = pl.program_id(0)
    m_sc[...] = jnp.full(m_sc.shape, NEG_INF, F32)
    l_sc[...] = jnp.zeros(l_sc.shape, F32)
    acc_sc[...] = jnp.zeros(acc_sc.shape, F32)

    def chunk(j, masked):
        kc = k_ref[pl.ds(pl.multiple_of(j * t, t), t), :]
        vt = vt_ref[j]
        if masked:
            key = lax.broadcasted_iota(jnp.int32, (t, t), 0)
            qry = lax.broadcasted_iota(jnp.int32, (t, t), 1)
            keep = key <= qry
        st_next = _dot_nt(kc, q_ref[0])
        for hd in range(N_HEADS):
            st = st_next
            if hd + 1 < N_HEADS:
                st_next = _dot_nt(kc, q_ref[hd + 1])
            if masked:
                st = jnp.where(keep, st, NEG_INF)
            m_prev = m_sc[hd]
            m_new = jnp.maximum(m_prev, jnp.max(st, axis=0, keepdims=True))
            alpha = jnp.exp2(m_prev - m_new)
            p = jnp.exp2(st - m_new)
            l_sc[hd] = alpha * l_sc[hd] + jnp.sum(p, axis=0, keepdims=True)
            acc_sc[hd] = alpha * acc_sc[hd] + _dot(vt, p.astype(BF16))
            m_sc[hd] = m_new

    def body(j, carry):
        chunk(j, False)
        return carry

    lax.fori_loop(0, i, body, 0)
    chunk(i, True)
    for hd in range(N_HEADS):
        o = acc_sc[hd] / l_sc[hd]
        o_ref[:, hd * KV_RANK:(hd + 1) * KV_RANK] = o.T.astype(BF16)


def _epilogue_kernel(o_ref, sz_ref, ga_ref, yp_ref, x_ref, w_uv_ref, w_oatt_ref, w_out_ref,
                     g_final_ref, y_ref):
    o = o_ref[...]
    o2 = jnp.concatenate(
        [_dot(o[:, hd * KV_RANK:(hd + 1) * KV_RANK], w_uv_ref[hd]) for hd in range(N_HEADS)], axis=1)
    y_att = _dot((o2 * sz_ref[...]).astype(BF16), w_oatt_ref[...])
    merged = ga_ref[...] * y_att + yp_ref[...]
    xo = x_ref[...] + _dot(merged.astype(BF16), w_out_ref[...])
    y_ref[...] = _rms(xo, g_final_ref[...])


def _sample_proj_kernel(x_ref, cs_ref, hist_ref, g_pre_ref, w_a_ref, w_za_ref, w_up_ref,
                        w_zp_ref, w_ga_ref, w_gp_ref, g_q_ref, w_qn_ref, w_qr_ref, w_ukt_ref,
                        g_kv_ref, w_pool_ref, pscale_ref, w_opool_ref,
                        qa_ref, qr_ref, ckv_ref, kr_ref, sz_ref, ga_ref, yp_ref, pstate_ref,
                        *, past_len):
    cs = cs_ref[...]
    h, qa, qr, ckv, kr2 = _project(x_ref[...], cs, g_pre_ref[...], w_a_ref[...], g_q_ref[...],
                                   w_qn_ref[...], w_qr_ref[...], w_ukt_ref, g_kv_ref[...])
    for hd in range(N_HEADS):
        qa_ref[:, hd * LANES:(hd + 1) * LANES] = qa[hd]
        qr_ref[:, hd * LANES:(hd + 1) * LANES] = qr[hd]
    ckv_ref[...] = ckv
    kr_ref[...] = kr2[:, :QK_ROPE]
    sz_ref[...] = _silu(_dot(h, w_za_ref[...]))
    ga_ref[...] = jax.nn.sigmoid(_dot(h, w_ga_ref[...]))

    u = _dot(h, w_up_ref[...])
    win_sum, rcnt = [], []
    for g, w in enumerate(POOL_WINDOWS):
        c0 = g * POOL_GROUP_DIM
        acc = u[:, c0:c0 + POOL_GROUP_DIM]
        for s in range(1, w):
            off = (POOL_HIST - s) * POOL_WIDTH + c0
            acc = acc + hist_ref[:, off:off + POOL_GROUP_DIM]
        win_sum.append(acc)
        rcnt.append(1.0 / float(min(w, past_len + 1)))
    yp_ref[...] = _pool_branch(h, win_sum, u, rcnt, w_zp_ref[...], w_gp_ref[...], w_pool_ref,
                               pscale_ref[...], w_opool_ref[...])
    keep = (POOL_HIST - 1) * POOL_WIDTH
    pstate_ref[:, 0:keep] = hist_ref[:, POOL_WIDTH:]
    pstate_ref[:, keep:] = u


def _lane_group_reduce(v, op):
    sh = N_HEADS
    while sh < LANES:
        v = op(v, pltpu.roll(v, sh, axis=1))
        sh *= 2
    return v


def _diag_to_col(v):
    sub = lax.broadcasted_iota(jnp.int32, v.shape, 0)
    lane = lax.broadcasted_iota(jnp.int32, v.shape, 1)
    return jnp.sum(jnp.where(sub == lane, v, 0.0), axis=1, keepdims=True)


def _latent_rows(kbuf, slot):
    n_pairs, _, page, width = kbuf.shape[1:]
    halves = []
    for half in range(2):
        cols = []
        for jj in range(KEYS_PER_ROW // 2):
            r0 = (half * (KEYS_PER_ROW // 2) + jj) * 8
            for par in range(2):
                cols.append(kbuf[slot, :, par, r0:r0 + 8, :].reshape(n_pairs * 8, width))
        halves.append(jnp.concatenate(cols, axis=1))
    return jnp.concatenate(halves, axis=0).astype(BF16)


def _rotary_rows(rbuf, slot):
    n_pairs = rbuf.shape[1]
    xt = jnp.swapaxes(rbuf[slot], 1, 2)
    halves = []
    for half in range(2):
        cols = []
        for jj in range(KEYS_PER_ROW // 2):
            r0 = (half * (KEYS_PER_ROW // 2) + jj) * 8
            cols.append(xt[:, r0:r0 + 8, :].reshape(n_pairs * 8, LANES))
        halves.append(jnp.concatenate(cols, axis=1))
    return jnp.concatenate(halves, axis=0).astype(BF16)


def _sample_attn_kernel(pt_ref, qat_ref, qrt_ref, knew_ref, rnew_ref, ck_hbm, krt_hbm, o_ref,
                        kbuf, rbuf, sems, *, n_pages):
    b = pl.program_id(0)
    nb = pl.num_programs(0)
    slot = b % 2

    def copies(bb, sl, q):
        out = []
        for par in range(2):
            pid = pt_ref[bb * n_pages + 2 * q + par]
            out.append(pltpu.make_async_copy(ck_hbm.at[pid], kbuf.at[sl, q, par], sems.at[0, sl]))
            out.append(pltpu.make_async_copy(
                krt_hbm.at[pid], rbuf.at[sl, q, par * QK_ROPE:(par + 1) * QK_ROPE, :],
                sems.at[1, sl]))
        return out

    def start_all(bb, sl):
        def body(q, c):
            for cp in copies(bb, sl, q):
                cp.start()
            return c
        lax.fori_loop(0, n_pages // 2, body, 0)

    def wait_all(bb, sl):
        def body(q, c):
            for cp in copies(bb, sl, q):
                cp.wait()
            return c
        lax.fori_loop(0, n_pages // 2, body, 0)

    @pl.when(b == 0)
    def _():
        start_all(b, slot)

    @pl.when(b + 1 < nb)
    def _():
        start_all(b + 1, 1 - slot)

    wait_all(b, slot)

    qat = qat_ref[0]
    qrt = qrt_ref[0]
    lane_a = lax.broadcasted_iota(jnp.int32, qat.shape, 1) // N_HEADS
    lane_r = lax.broadcasted_iota(jnp.int32, qrt.shape, 1) // N_HEADS
    zero = jnp.zeros((), BF16)
    wk = jnp.concatenate([jnp.where(lane_a == j, qat, zero) for j in range(KEYS_PER_ROW)], axis=0)
    wr = jnp.concatenate([jnp.where(lane_r == j, qrt, zero) for j in range(KEYS_PER_ROW)], axis=0)

    kb = _latent_rows(kbuf, slot)
    rb = _rotary_rows(rbuf, slot)
    s2 = _dot(kb, wk) + _dot(rb, wr)

    knew = knew_ref[0].astype(BF16)
    rnew = rnew_ref[0].astype(BF16)
    s_new = (_dot(jnp.broadcast_to(knew, (8, KV_RANK)), qat)
             + _dot(jnp.broadcast_to(rnew, (8, QK_ROPE)), qrt))

    m = jnp.max(s2, axis=0, keepdims=True)
    m = _lane_group_reduce(jnp.broadcast_to(m, (8, LANES)), jnp.maximum)
    m = jnp.maximum(m, s_new)
    p2 = jnp.exp2(s2 - m[0:1])
    p_new = jnp.exp2(s_new - m)
    l_lane = _lane_group_reduce(
        jnp.broadcast_to(jnp.sum(p2, axis=0, keepdims=True), (8, LANES)), jnp.add)
    p_new_col = _diag_to_col(p_new)
    l_col = _diag_to_col(l_lane) + p_new_col

    g = _dot(p2.T.astype(BF16), kb)
    o = p_new_col.astype(BF16).astype(F32) * knew.astype(F32)
    for j in range(KEYS_PER_ROW):
        o = o + g[j * N_HEADS:(j + 1) * N_HEADS, j * KV_RANK:(j + 1) * KV_RANK]
    o_ref[0] = o / l_col


def _const_spec(shape):
    zeros = (0,) * len(shape)
    return pl.BlockSpec(shape, lambda *_: zeros)


def _prep_weights(g_pre, w_in, g_q, w_uq, g_kv, w_uk, w_uv, w_o_att, w_pool, pool_scale,
                  w_o_pool, w_out, g_final):
    d_model = w_in.shape[0]
    att_w = N_HEADS * V_DIM
    half = QK_ROPE // 2
    o_kv = Q_RANK
    o_kr = o_kv + KV_RANK
    o_za = o_kr + QK_ROPE
    o_up = o_za + att_w
    o_zp = o_up + POOL_WIDTH
    o_ga = o_zp + POOL_WIDTH
    o_gp = o_ga + d_model
    w_kr = w_in[:, o_kr:o_za]
    w_a = jnp.concatenate([w_in[:, :o_kr], w_kr, -w_kr[:, half:], w_kr[:, :half]], axis=1)
    uq = w_uq.reshape(Q_RANK, N_HEADS, QK_NOPE + QK_ROPE)
    w_qn = uq[:, :, :QK_NOPE].reshape(Q_RANK, N_HEADS * QK_NOPE)
    x1, x2 = uq[:, :, QK_NOPE:QK_NOPE + half], uq[:, :, QK_NOPE + half:]
    w_qr = jnp.concatenate([x1, x2, -x2, x1], axis=2).reshape(Q_RANK, N_HEADS * LANES)
    bf = lambda a: a.astype(BF16)
    row = lambda a: a.reshape(1, -1).astype(F32)
    return dict(
        g_pre=row(g_pre), w_a=bf(w_a), w_za=bf(w_in[:, o_za:o_up]), w_up=bf(w_in[:, o_up:o_zp]),
        w_zp=bf(w_in[:, o_zp:o_ga]), w_ga=bf(w_in[:, o_ga:o_gp]), w_gp=bf(w_in[:, o_gp:]),
        g_q=row(g_q), w_qn=bf(w_qn), w_qr=bf(w_qr),
        w_ukt=bf(jnp.transpose(w_uk, (1, 2, 0))),
        g_kv=row(g_kv), w_pool=bf(w_pool), pscale=row(pool_scale), w_opool=bf(w_o_pool),
        w_uv=bf(jnp.transpose(w_uv, (1, 0, 2))),
        w_oatt=bf(w_o_att), w_out=bf(w_out), g_final=row(g_final))


_PROJ_WEIGHTS = ("g_pre", "w_a", "w_za", "w_up", "w_zp", "w_ga", "w_gp", "g_q", "w_qn", "w_qr",
                 "w_ukt", "g_kv", "w_pool", "pscale", "w_opool")


def _rope_table(pos):
    half = QK_ROPE // 2
    inv = ROPE_BASE ** (-jnp.arange(half, dtype=F32) / half)
    ang = pos.astype(F32)[:, None] * inv[None, :]
    c, s = jnp.cos(ang), jnp.sin(ang)
    return jnp.concatenate([c, c, s, s], axis=1)


def _prompt_proj(x, cs, hist, wts, tm):
    seq, d_model = x.shape
    n = seq // tm
    hist_rows = hist.shape[0]
    ws = [wts[k] for k in _PROJ_WEIGHTS]
    rows = lambda width: pl.BlockSpec((tm, width), lambda i: (i, 0))
    out_shape = (
        jax.ShapeDtypeStruct((N_HEADS, seq, QCAT), BF16),
        jax.ShapeDtypeStruct((seq, QCAT), BF16),
        jax.ShapeDtypeStruct((n, KV_RANK, tm), BF16),
        jax.ShapeDtypeStruct((seq, KV_RANK), F32),
        jax.ShapeDtypeStruct((seq, QK_ROPE), F32),
        jax.ShapeDtypeStruct((seq, d_model), F32),
        jax.ShapeDtypeStruct((seq, d_model), F32),
        jax.ShapeDtypeStruct((seq, d_model), F32),
        jax.ShapeDtypeStruct((hist_rows, POOL_WIDTH), F32),
    )
    out_specs = (
        pl.BlockSpec((N_HEADS, tm, QCAT), lambda i: (0, i, 0)),
        rows(QCAT),
        pl.BlockSpec((1, KV_RANK, tm), lambda i: (i, 0, 0)),
        rows(KV_RANK), rows(QK_ROPE), rows(d_model), rows(d_model), rows(d_model),
        _const_spec((hist_rows, POOL_WIDTH)),
    )
    return pl.pallas_call(
        functools.partial(_prompt_proj_kernel, tm=tm),
        grid=(n,),
        in_specs=[rows(d_model), rows(LANES), _const_spec(hist.shape)]
        + [_const_spec(w.shape) for w in ws],
        out_specs=out_specs,
        out_shape=out_shape,
        scratch_shapes=[pltpu.VMEM((hist_rows + tm, POOL_WIDTH), F32)],
        compiler_params=pltpu.CompilerParams(
            dimension_semantics=("arbitrary",), vmem_limit_bytes=VMEM_LIMIT),
    )(x, cs, hist, *ws)


def _prompt_attn(qcat, kcat, vt, t):
    seq = kcat.shape[0]
    n = seq // t
    return pl.pallas_call(
        functools.partial(_prompt_attn_kernel, t=t),
        grid=(n,),
        in_specs=[pl.BlockSpec((N_HEADS, t, QCAT), lambda i: (0, i, 0)),
                  pl.BlockSpec(memory_space=pltpu.VMEM),
                  pl.BlockSpec(memory_space=pltpu.VMEM)],
        out_specs=pl.BlockSpec((t, N_HEADS * KV_RANK), lambda i: (i, 0)),
        out_shape=jax.ShapeDtypeStruct((seq, N_HEADS * KV_RANK), BF16),
        scratch_shapes=[pltpu.VMEM((N_HEADS, 1, t), F32),
                        pltpu.VMEM((N_HEADS, 1, t), F32),
                        pltpu.VMEM((N_HEADS, KV_RANK, t), F32)],
        compiler_params=pltpu.CompilerParams(
            dimension_semantics=("arbitrary",), vmem_limit_bytes=VMEM_LIMIT),
    )(qcat, kcat, vt)


def _epilogue(o, sz, ga, yp, x, wts, tm):
    n_rows, d_model = x.shape
    rows = pl.BlockSpec((tm, d_model), lambda i: (i, 0))
    ws = [wts[k] for k in ("w_uv", "w_oatt", "w_out", "g_final")]
    return pl.pallas_call(
        _epilogue_kernel,
        grid=(n_rows // tm,),
        in_specs=[rows] * 5 + [_const_spec(w.shape) for w in ws],
        out_specs=rows,
        out_shape=jax.ShapeDtypeStruct((n_rows, d_model), F32),
        compiler_params=pltpu.CompilerParams(
            dimension_semantics=("arbitrary",), vmem_limit_bytes=VMEM_LIMIT),
    )(o, sz, ga, yp, x, *ws)


def _sample_proj(x, cs, hist2d, wts, past_len):
    b, d_model = x.shape
    ws = [wts[k] for k in _PROJ_WEIGHTS]
    wide = jax.ShapeDtypeStruct((b, d_model), F32)
    out_shape = (
        jax.ShapeDtypeStruct((b, N_HEADS * LANES), F32),
        jax.ShapeDtypeStruct((b, N_HEADS * LANES), F32),
        jax.ShapeDtypeStruct((b, KV_RANK), F32),
        jax.ShapeDtypeStruct((b, QK_ROPE), F32),
        wide, wide, wide,
        jax.ShapeDtypeStruct(hist2d.shape, F32),
    )
    args = (x, cs, hist2d, *ws)
    return pl.pallas_call(
        functools.partial(_sample_proj_kernel, past_len=past_len),
        grid=(1,),
        in_specs=[_const_spec(a.shape) for a in args],
        out_specs=tuple(_const_spec(s.shape) for s in out_shape),
        out_shape=out_shape,
        compiler_params=pltpu.CompilerParams(
            dimension_semantics=("arbitrary",), vmem_limit_bytes=VMEM_LIMIT),
    )(*args)


def _sample_attn(page_table, qat, qrt, knew, rnew, cache_ckv, cache_krope):
    b, n_pages = page_table.shape
    page = cache_ckv.shape[1]
    per_batch = lambda rows, width: pl.BlockSpec((1, rows, width), lambda i, pt: (i, 0, 0))
    grid_spec = pltpu.PrefetchScalarGridSpec(
        num_scalar_prefetch=1,
        grid=(b,),
        in_specs=[per_batch(KV_RANK, LANES), per_batch(QK_ROPE, LANES),
                  per_batch(1, KV_RANK), per_batch(1, QK_ROPE),
                  pl.BlockSpec(memory_space=pl.ANY),
                  pl.BlockSpec(memory_space=pl.ANY)],
        out_specs=per_batch(N_HEADS, KV_RANK),
        scratch_shapes=[pltpu.VMEM((2, n_pages // 2, 2, page, KV_RANK), F32),
                        pltpu.VMEM((2, n_pages // 2, 2 * QK_ROPE, page), F32),
                        pltpu.SemaphoreType.DMA((2, 2))],
    )
    return pl.pallas_call(
        functools.partial(_sample_attn_kernel, n_pages=n_pages),
        grid_spec=grid_spec,
        out_shape=jax.ShapeDtypeStruct((b, N_HEADS, KV_RANK), F32),
        compiler_params=pltpu.CompilerParams(
            dimension_semantics=("arbitrary",), vmem_limit_bytes=VMEM_LIMIT),
    )(page_table.reshape(-1), qat, qrt, knew, rnew, cache_ckv, jnp.swapaxes(cache_krope, 1, 2))


def kernel(x_prompt, x_sample, cache_ckv, cache_krope, state_pool, page_table, g_pre, w_in, g_q,
           w_uq, g_kv, w_uk, w_uv, w_o_att, w_pool, pool_scale, w_o_pool, w_out, g_final):
    depth = g_pre.shape[0]
    batch, seq, d_model = x_prompt.shape
    dec_b, dec_seq, _ = x_sample.shape
    assert depth == 1 and batch == 1 and dec_seq == 1
    past_len = page_table.shape[1] * cache_ckv.shape[2]
    wts = _prep_weights(g_pre[0], w_in[0], g_q[0], w_uq[0], g_kv[0], w_uk[0], w_uv[0], w_o_att[0],
                        w_pool[0], pool_scale[0], w_o_pool[0], w_out[0], g_final)

    tile = 512
    hist_rows = 16
    xp = x_prompt.reshape(seq, d_model)
    cs_p = _rope_table(jnp.arange(seq, dtype=jnp.int32))
    hist0 = jnp.zeros((hist_rows, POOL_WIDTH), F32)
    qcat, kcat, vt, ckv_p, kr_p, sz, ga, yp, pstate = _prompt_proj(xp, cs_p, hist0, wts, tile)
    o_p = _prompt_attn(qcat, kcat, vt, tile)
    y_p = _epilogue(o_p, sz, ga, yp, xp, wts, tile)

    xs = x_sample.reshape(dec_b, d_model)
    cs_s = _rope_table(jnp.full((dec_b,), past_len, jnp.int32))
    hist2d = state_pool[0].reshape(dec_b, POOL_HIST * POOL_WIDTH)
    qa, qr, ckv_s, kr_s, sz_s, ga_s, yp_s, pstate_s = _sample_proj(xs, cs_s, hist2d, wts, past_len)
    qa3 = qa.reshape(dec_b, N_HEADS, LANES)
    qr3 = qr.reshape(dec_b, N_HEADS, LANES)[:, :, :QK_ROPE]
    qat = jnp.tile(jnp.transpose(qa3, (0, 2, 1)), (1, 1, KEYS_PER_ROW)).astype(BF16)
    qrt = jnp.tile(jnp.transpose(qr3, (0, 2, 1)), (1, 1, KEYS_PER_ROW)).astype(BF16)
    o_s = _sample_attn(page_table, qat, qrt, ckv_s.reshape(dec_b, 1, KV_RANK),
                       kr_s.reshape(dec_b, 1, QK_ROPE), cache_ckv[0], cache_krope[0])
    o_s = o_s.reshape(dec_b, N_HEADS * KV_RANK).astype(BF16)
    y_s = _epilogue(o_s, sz_s, ga_s, yp_s, xs, wts, dec_b)

    return (y_p.reshape(batch, seq, d_model),
            y_s.reshape(dec_b, 1, d_model),
            ckv_p.reshape(1, batch, seq, KV_RANK),
            kr_p.reshape(1, batch, seq, QK_ROPE),
            pstate[hist_rows - POOL_HIST:].reshape(1, batch, POOL_HIST, POOL_WIDTH),
            ckv_s.reshape(1, dec_b, 1, KV_RANK),
            kr_s.reshape(1, dec_b, 1, QK_ROPE),
            pstate_s.reshape(1, dec_b, POOL_HIST, POOL_WIDTH))
```

```python
import functools

import jax
import jax.numpy as jnp
from jax import lax
from jax.experimental import pallas as pl
from jax.experimental.pallas import tpu as pltpu

F32 = jnp.float32
BF16 = jnp.bfloat16

N_HEADS = 8
QK_NOPE = 128
QK_ROPE = 64
V_DIM = 128
Q_RANK = 256
KV_RANK = 128
ROPE_BASE = 10000.0
POOL_WINDOWS = (2, 4, 8, 16)
POOL_GROUP_DIM = 128
POOL_WIDTH = len(POOL_WINDOWS) * POOL_GROUP_DIM
POOL_HIST = 15
EPS = 1e-6
NEG_INF = -1e30
SOFTMAX_SCALE = (QK_NOPE + QK_ROPE) ** -0.5
LOG2_E = 1.4426950408889634
Q_SCALE = SOFTMAX_SCALE * LOG2_E
SAFE_SLACK = 64.0

LANES = 128
KEYS_PER_ROW = LANES // N_HEADS
QCAT = 2 * LANES
VMEM_LIMIT = 56 * 1024 * 1024


def _dot(a, b):
    return jnp.dot(a, b, preferred_element_type=F32)


def _dot_nt(a, b):
    return lax.dot_general(a, b, (((1,), (1,)), ((), ())), preferred_element_type=F32)


def _rms(x, g):
    return x * lax.rsqrt(jnp.mean(x * x, axis=-1, keepdims=True) + EPS) * g


def _silu(x):
    return x * jax.nn.sigmoid(x)


def _rope_pair(t):
    return t + pltpu.roll(t, QK_ROPE, axis=1)


def _project(x, cs, g_pre, w_a, g_q, w_qn, w_qr, w_ukt, g_kv):
    h = _rms(x, g_pre).astype(BF16)
    za = _dot(h, w_a)
    qn = _rms(za[:, :Q_RANK], g_q).astype(BF16)
    ckv = _rms(za[:, Q_RANK:Q_RANK + KV_RANK], g_kv)
    kr2 = _rope_pair(za[:, Q_RANK + KV_RANK:] * cs)
    q_nope = _dot(qn, w_qn)
    q_rope = _dot(qn, w_qr)
    qa, qr = [], []
    for hd in range(N_HEADS):
        sl = slice(hd * LANES, (hd + 1) * LANES)
        qa.append(_dot(q_nope[:, sl].astype(BF16), w_ukt[hd]) * Q_SCALE)
        qr.append(_rope_pair(q_rope[:, sl] * cs) * Q_SCALE)
    return h, qa, qr, ckv, kr2


def _pool_branch(h, win_sum, u, rcnt, w_zp, w_gp, w_pool, pool_scale, w_o_pool):
    mixed = []
    for g in range(len(POOL_WINDOWS)):
        sl = slice(g * POOL_GROUP_DIM, (g + 1) * POOL_GROUP_DIM)
        pooled = win_sum[g] * rcnt[g] - u[:, sl]
        mixed.append(_dot(pooled.astype(BF16), w_pool[g]))
    p = jnp.concatenate(mixed, axis=1) * pool_scale
    z_pool = _dot(h, w_zp)
    y_pool = _dot((p * _silu(z_pool)).astype(BF16), w_o_pool)
    return jax.nn.sigmoid(_dot(h, w_gp)) * y_pool


def _prompt_proj_kernel(x_ref, cs_ref, hist_ref, g_pre_ref, w_a_ref, w_za_ref, w_up_ref,
                        w_zp_ref, w_ga_ref, w_gp_ref, g_q_ref, w_qn_ref, w_qr_ref, w_ukt_ref,
                        g_kv_ref, w_pool_ref, pscale_ref, w_opool_ref,
                        qcat_ref, kcat_ref, vt_ref, ckv_ref, kr_ref, sz_ref, ga_ref, yp_ref,
                        pstate_ref, slack_ref, uext_ref, kmax_ref, *, tm):
    i = pl.program_id(0)
    hist_rows = uext_ref.shape[0] - tm

    @pl.when(i == 0)
    def _():
        uext_ref[0:hist_rows, :] = hist_ref[...]
        kmax_ref[...] = jnp.zeros(kmax_ref.shape, F32)

    cs = cs_ref[...]
    h, qa, qr, ckv, kr2 = _project(x_ref[...], cs, g_pre_ref[...], w_a_ref[...], g_q_ref[...],
                                   w_qn_ref[...], w_qr_ref[...], w_ukt_ref, g_kv_ref[...])
    lane = lax.broadcasted_iota(jnp.int32, kr2.shape, 1)
    is_rope = lane < QK_ROPE
    is_shift = lane == QK_ROPE
    rsum = lambda a: jnp.sum(a, axis=1, keepdims=True)
    ckv_b = ckv.astype(BF16).astype(F32)
    kr_b = jnp.where(is_rope, kr2, 0.0).astype(BF16).astype(F32)
    k_norm2 = jnp.max(rsum(ckv_b * ckv_b + kr_b * kr_b), axis=0, keepdims=True)
    k_norm2 = jnp.maximum(k_norm2, kmax_ref[0:1, 0:1])
    kmax_ref[...] = jnp.broadcast_to(k_norm2, kmax_ref.shape)
    slack = None
    for hd in range(N_HEADS):
        qa_b = qa[hd].astype(BF16)
        qr_b = jnp.where(is_rope, qr[hd], 0.0).astype(BF16).astype(F32)
        qa_f = qa_b.astype(F32)
        c = jnp.sqrt(rsum(qa_f * qa_f + qr_b * qr_b) * k_norm2) + 1.0
        c = c.astype(BF16).astype(F32)
        own = rsum(qa_f * ckv_b + qr_b * kr_b)
        worst = jnp.max(c - own, axis=0, keepdims=True)
        slack = worst if slack is None else jnp.maximum(slack, worst)
        qcat_ref[hd, :, 0:LANES] = qa_b
        qcat_ref[hd, :, LANES:QCAT] = jnp.where(is_shift, -c, qr_b).astype(BF16)
    slack_ref[0] = jnp.broadcast_to(slack, slack_ref.shape[1:])
    kcat_ref[:, 0:LANES] = ckv.astype(BF16)
    kcat_ref[:, LANES:QCAT] = jnp.where(is_shift, 1.0, kr_b).astype(BF16)
    vt_ref[0] = ckv.T.astype(BF16)
    ckv_ref[...] = ckv
    kr_ref[...] = kr2[:, :QK_ROPE]

    sz_ref[...] = _silu(_dot(h, w_za_ref[...]))
    ga_ref[...] = jax.nn.sigmoid(_dot(h, w_ga_ref[...]))

    u = _dot(h, w_up_ref[...])
    uext_ref[hist_rows:, :] = u
    pos = i * tm + lax.broadcasted_iota(jnp.int32, (tm, 1), 0)
    win_sum, rcnt = [], []
    for g, w in enumerate(POOL_WINDOWS):
        c0 = g * POOL_GROUP_DIM
        acc = u[:, c0:c0 + POOL_GROUP_DIM]
        for s in range(1, w):
            acc = acc + uext_ref[hist_rows - s:hist_rows - s + tm, c0:c0 + POOL_GROUP_DIM]
        win_sum.append(acc)
        rcnt.append(1.0 / jnp.minimum(w, pos + 1).astype(F32))
    yp_ref[...] = _pool_branch(h, win_sum, u, rcnt, w_zp_ref[...], w_gp_ref[...], w_pool_ref,
                               pscale_ref[...], w_opool_ref[...])
    tail = uext_ref[tm:tm + hist_rows, :]
    pstate_ref[...] = tail
    uext_ref[0:hist_rows, :] = tail


def _prompt_attn_kernel(safe_ref, q_ref, k_ref, vt_ref, o_ref, m_sc, l_sc, acc_sc, *, t):
    i = pl.program_id(0)
    l_sc[...] = jnp.zeros(l_sc.shape, F32)
    acc_sc[...] = jnp.zeros(acc_sc.shape, F32)

    def chunk(j, masked, shifted):
        kc = k_ref[pl.ds(pl.multiple_of(j * t, t), t), :]
        vt = vt_ref[j]
        if masked:
            key = lax.broadcasted_iota(jnp.int32, (t, t), 0)
            qry = lax.broadcasted_iota(jnp.int32, (t, t), 1)
            keep = key <= qry
        st_next = _dot_nt(kc, q_ref[0])
        for hd in range(N_HEADS):
            st = st_next
            if hd + 1 < N_HEADS:
                st_next = _dot_nt(kc, q_ref[hd + 1])
            if masked:
                st = jnp.where(keep, st, NEG_INF)
            if shifted:
                p = jnp.exp2(st)
                l_sc[hd] += jnp.sum(p, axis=0, keepdims=True)
                acc_sc[hd] += _dot(vt, p.astype(BF16))
            else:
                m_prev = m_sc[hd]
                m_new = jnp.maximum(m_prev, jnp.max(st, axis=0, keepdims=True))
                alpha = jnp.exp2(m_prev - m_new)
                p = jnp.exp2(st - m_new)
                l_sc[hd] = alpha * l_sc[hd] + jnp.sum(p, axis=0, keepdims=True)
                acc_sc[hd] = alpha * acc_sc[hd] + _dot(vt, p.astype(BF16))
                m_sc[hd] = m_new

    def sweep(shifted):
        def body(j, carry):
            chunk(j, False, shifted)
            return carry
        lax.fori_loop(0, i, body, 0)
        chunk(i, True, shifted)

    safe = safe_ref[i] != 0

    @pl.when(safe)
    def _():
        sweep(True)

    @pl.when(jnp.logical_not(safe))
    def _():
        m_sc[...] = jnp.full(m_sc.shape, NEG_INF, F32)
        sweep(False)

    for hd in range(N_HEADS):
        o = acc_sc[hd] / l_sc[hd]
        o_ref[:, hd * KV_RANK:(hd + 1) * KV_RANK] = o.T.astype(BF16)


def _epilogue_kernel(o_ref, sz_ref, ga_ref, yp_ref, x_ref, w_uv_ref, w_oatt_ref, w_out_ref,
                     g_final_ref, y_ref):
    o = o_ref[...]
    o2 = jnp.concatenate(
        [_dot(o[:, hd * KV_RANK:(hd + 1) * KV_RANK], w_uv_ref[hd]) for hd in range(N_HEADS)], axis=1)
    y_att = _dot((o2 * sz_ref[...]).astype(BF16), w_oatt_ref[...])
    merged = ga_ref[...] * y_att + yp_ref[...]
    xo = x_ref[...] + _dot(merged.astype(BF16), w_out_ref[...])
    y_ref[...] = _rms(xo, g_final_ref[...])


def _sample_proj_kernel(x_ref, cs_ref, hist_ref, g_pre_ref, w_a_ref, w_za_ref, w_up_ref,
                        w_zp_ref, w_ga_ref, w_gp_ref, g_q_ref, w_qn_ref, w_qr_ref, w_ukt_ref,
                        g_kv_ref, w_pool_ref, pscale_ref, w_opool_ref,
                        qa_ref, qr_ref, ckv_ref, kr_ref, sz_ref, ga_ref, yp_ref, pstate_ref,
                        *, past_len):
    cs = cs_ref[...]
    h, qa, qr, ckv, kr2 = _project(x_ref[...], cs, g_pre_ref[...], w_a_ref[...], g_q_ref[...],
                                   w_qn_ref[...], w_qr_ref[...], w_ukt_ref, g_kv_ref[...])
    for hd in range(N_HEADS):
        qa_ref[:, hd * LANES:(hd + 1) * LANES] = qa[hd]
        qr_ref[:, hd * LANES:(hd + 1) * LANES] = qr[hd]
    ckv_ref[...] = ckv
    kr_ref[...] = kr2[:, :QK_ROPE]
    sz_ref[...] = _silu(_dot(h, w_za_ref[...]))
    ga_ref[...] = jax.nn.sigmoid(_dot(h, w_ga_ref[...]))

    u = _dot(h, w_up_ref[...])
    win_sum, rcnt = [], []
    for g, w in enumerate(POOL_WINDOWS):
        c0 = g * POOL_GROUP_DIM
        acc = u[:, c0:c0 + POOL_GROUP_DIM]
        for s in range(1, w):
            off = (POOL_HIST - s) * POOL_WIDTH + c0
            acc = acc + hist_ref[:, off:off + POOL_GROUP_DIM]
        win_sum.append(acc)
        rcnt.append(1.0 / float(min(w, past_len + 1)))
    yp_ref[...] = _pool_branch(h, win_sum, u, rcnt, w_zp_ref[...], w_gp_ref[...], w_pool_ref,
                               pscale_ref[...], w_opool_ref[...])
    keep = (POOL_HIST - 1) * POOL_WIDTH
    pstate_ref[:, 0:keep] = hist_ref[:, POOL_WIDTH:]
    pstate_ref[:, keep:] = u


def _lane_group_reduce(v, op):
    sh = N_HEADS
    while sh < LANES:
        v = op(v, pltpu.roll(v, sh, axis=1))
        sh *= 2
    return v


def _diag_to_col(v):
    sub = lax.broadcasted_iota(jnp.int32, v.shape, 0)
    lane = lax.broadcasted_iota(jnp.int32, v.shape, 1)
    return jnp.sum(jnp.where(sub == lane, v, 0.0), axis=1, keepdims=True)


def _latent_rows(kbuf, slot):
    n_pairs, _, page, width = kbuf.shape[1:]
    halves = []
    for half in range(2):
        cols = []
        for jj in range(KEYS_PER_ROW // 2):
            r0 = (half * (KEYS_PER_ROW // 2) + jj) * 8
            for par in range(2):
                cols.append(kbuf[slot, :, par, r0:r0 + 8, :].reshape(n_pairs * 8, width))
        halves.append(jnp.concatenate(cols, axis=1))
    return jnp.concatenate(halves, axis=0).astype(BF16)


def _rotary_rows(rbuf, slot):
    n_pairs = rbuf.shape[1]
    xt = jnp.swapaxes(rbuf[slot], 1, 2)
    halves = []
    for half in range(2):
        cols = []
        for jj in range(KEYS_PER_ROW // 2):
            r0 = (half * (KEYS_PER_ROW // 2) + jj) * 8
            cols.append(xt[:, r0:r0 + 8, :].reshape(n_pairs * 8, LANES))
        halves.append(jnp.concatenate(cols, axis=1))
    return jnp.concatenate(halves, axis=0).astype(BF16)


def _sample_attn_kernel(pt_ref, qat_ref, qrt_ref, knew_ref, rnew_ref, ck_hbm, krt_hbm, o_ref,
                        kbuf, rbuf, sems, *, n_pages):
    b = pl.program_id(0)
    nb = pl.num_programs(0)
    slot = b % 2

    def copies(bb, sl, q):
        out = []
        for par in range(2):
            pid = pt_ref[bb * n_pages + 2 * q + par]
            out.append(pltpu.make_async_copy(ck_hbm.at[pid], kbuf.at[sl, q, par], sems.at[0, sl]))
            out.append(pltpu.make_async_copy(
                krt_hbm.at[pid], rbuf.at[sl, q, par * QK_ROPE:(par + 1) * QK_ROPE, :],
                sems.at[1, sl]))
        return out

    def start_all(bb, sl):
        def body(q, c):
            for cp in copies(bb, sl, q):
                cp.start()
            return c
        lax.fori_loop(0, n_pages // 2, body, 0)

    def wait_all(bb, sl):
        def body(q, c):
            for cp in copies(bb, sl, q):
                cp.wait()
            return c
        lax.fori_loop(0, n_pages // 2, body, 0)

    @pl.when(b == 0)
    def _():
        start_all(b, slot)

    @pl.when(b + 1 < nb)
    def _():
        start_all(b + 1, 1 - slot)

    wait_all(b, slot)

    qat = qat_ref[0]
    qrt = qrt_ref[0]
    lane_a = lax.broadcasted_iota(jnp.int32, qat.shape, 1) // N_HEADS
    lane_r = lax.broadcasted_iota(jnp.int32, qrt.shape, 1) // N_HEADS
    zero = jnp.zeros((), BF16)
    wk = jnp.concatenate([jnp.where(lane_a == j, qat, zero) for j in range(KEYS_PER_ROW)], axis=0)
    wr = jnp.concatenate([jnp.where(lane_r == j, qrt, zero) for j in range(KEYS_PER_ROW)], axis=0)

    kb = _latent_rows(kbuf, slot)
    rb = _rotary_rows(rbuf, slot)
    s2 = _dot(kb, wk) + _dot(rb, wr)

    knew = knew_ref[0].astype(BF16)
    rnew = rnew_ref[0].astype(BF16)
    s_new = (_dot(jnp.broadcast_to(knew, (8, KV_RANK)), qat)
             + _dot(jnp.broadcast_to(rnew, (8, QK_ROPE)), qrt))

    m = jnp.max(s2, axis=0, keepdims=True)
    m = _lane_group_reduce(jnp.broadcast_to(m, (8, LANES)), jnp.maximum)
    m = jnp.maximum(m, s_new)
    p2 = jnp.exp2(s2 - m[0:1])
    p_new = jnp.exp2(s_new - m)
    l_lane = _lane_group_reduce(
        jnp.broadcast_to(jnp.sum(p2, axis=0, keepdims=True), (8, LANES)), jnp.add)
    p_new_col = _diag_to_col(p_new)
    l_col = _diag_to_col(l_lane) + p_new_col

    g = _dot(p2.T.astype(BF16), kb)
    o = p_new_col.astype(BF16).astype(F32) * knew.astype(F32)
    for j in range(KEYS_PER_ROW):
        o = o + g[j * N_HEADS:(j + 1) * N_HEADS, j * KV_RANK:(j + 1) * KV_RANK]
    o_ref[0] = o / l_col


def _const_spec(shape):
    zeros = (0,) * len(shape)
    return pl.BlockSpec(shape, lambda *_: zeros)


def _prep_weights(g_pre, w_in, g_q, w_uq, g_kv, w_uk, w_uv, w_o_att, w_pool, pool_scale,
                  w_o_pool, w_out, g_final):
    d_model = w_in.shape[0]
    att_w = N_HEADS * V_DIM
    half = QK_ROPE // 2
    o_kv = Q_RANK
    o_kr = o_kv + KV_RANK
    o_za = o_kr + QK_ROPE
    o_up = o_za + att_w
    o_zp = o_up + POOL_WIDTH
    o_ga = o_zp + POOL_WIDTH
    o_gp = o_ga + d_model
    w_kr = w_in[:, o_kr:o_za]
    w_a = jnp.concatenate([w_in[:, :o_kr], w_kr, -w_kr[:, half:], w_kr[:, :half]], axis=1)
    uq = w_uq.reshape(Q_RANK, N_HEADS, QK_NOPE + QK_ROPE)
    w_qn = uq[:, :, :QK_NOPE].reshape(Q_RANK, N_HEADS * QK_NOPE)
    x1, x2 = uq[:, :, QK_NOPE:QK_NOPE + half], uq[:, :, QK_NOPE + half:]
    w_qr = jnp.concatenate([x1, x2, -x2, x1], axis=2).reshape(Q_RANK, N_HEADS * LANES)
    bf = lambda a: a.astype(BF16)
    row = lambda a: a.reshape(1, -1).astype(F32)
    return dict(
        g_pre=row(g_pre), w_a=bf(w_a), w_za=bf(w_in[:, o_za:o_up]), w_up=bf(w_in[:, o_up:o_zp]),
        w_zp=bf(w_in[:, o_zp:o_ga]), w_ga=bf(w_in[:, o_ga:o_gp]), w_gp=bf(w_in[:, o_gp:]),
        g_q=row(g_q), w_qn=bf(w_qn), w_qr=bf(w_qr),
        w_ukt=bf(jnp.transpose(w_uk, (1, 2, 0))),
        g_kv=row(g_kv), w_pool=bf(w_pool), pscale=row(pool_scale), w_opool=bf(w_o_pool),
        w_uv=bf(jnp.transpose(w_uv, (1, 0, 2))),
        w_oatt=bf(w_o_att), w_out=bf(w_out), g_final=row(g_final))


_PROJ_WEIGHTS = ("g_pre", "w_a", "w_za", "w_up", "w_zp", "w_ga", "w_gp", "g_q", "w_qn", "w_qr",
                 "w_ukt", "g_kv", "w_pool", "pscale", "w_opool")


def _rope_table(pos):
    half = QK_ROPE // 2
    inv = ROPE_BASE ** (-jnp.arange(half, dtype=F32) / half)
    ang = pos.astype(F32)[:, None] * inv[None, :]
    c, s = jnp.cos(ang), jnp.sin(ang)
    return jnp.concatenate([c, c, s, s], axis=1)


def _prompt_proj(x, cs, hist, wts, tm):
    seq, d_model = x.shape
    n = seq // tm
    hist_rows = hist.shape[0]
    ws = [wts[k] for k in _PROJ_WEIGHTS]
    rows = lambda width: pl.BlockSpec((tm, width), lambda i: (i, 0))
    out_shape = (
        jax.ShapeDtypeStruct((N_HEADS, seq, QCAT), BF16),
        jax.ShapeDtypeStruct((seq, QCAT), BF16),
        jax.ShapeDtypeStruct((n, KV_RANK, tm), BF16),
        jax.ShapeDtypeStruct((seq, KV_RANK), F32),
        jax.ShapeDtypeStruct((seq, QK_ROPE), F32),
        jax.ShapeDtypeStruct((seq, d_model), F32),
        jax.ShapeDtypeStruct((seq, d_model), F32),
        jax.ShapeDtypeStruct((seq, d_model), F32),
        jax.ShapeDtypeStruct((hist_rows, POOL_WIDTH), F32),
        jax.ShapeDtypeStruct((n, 8, LANES), F32),
    )
    out_specs = (
        pl.BlockSpec((N_HEADS, tm, QCAT), lambda i: (0, i, 0)),
        rows(QCAT),
        pl.BlockSpec((1, KV_RANK, tm), lambda i: (i, 0, 0)),
        rows(KV_RANK), rows(QK_ROPE), rows(d_model), rows(d_model), rows(d_model),
        _const_spec((hist_rows, POOL_WIDTH)),
        pl.BlockSpec((1, 8, LANES), lambda i: (i, 0, 0)),
    )
    return pl.pallas_call(
        functools.partial(_prompt_proj_kernel, tm=tm),
        grid=(n,),
        in_specs=[rows(d_model), rows(LANES), _const_spec(hist.shape)]
        + [_const_spec(w.shape) for w in ws],
        out_specs=out_specs,
        out_shape=out_shape,
        scratch_shapes=[pltpu.VMEM((hist_rows + tm, POOL_WIDTH), F32),
                        pltpu.VMEM((8, LANES), F32)],
        compiler_params=pltpu.CompilerParams(
            dimension_semantics=("arbitrary",), vmem_limit_bytes=VMEM_LIMIT),
    )(x, cs, hist, *ws)


def _prompt_attn(safe, qcat, kcat, vt, t):
    seq = kcat.shape[0]
    n = seq // t
    grid_spec = pltpu.PrefetchScalarGridSpec(
        num_scalar_prefetch=1,
        grid=(n,),
        in_specs=[pl.BlockSpec((N_HEADS, t, QCAT), lambda i, s: (0, i, 0)),
                  pl.BlockSpec(memory_space=pltpu.VMEM),
                  pl.BlockSpec(memory_space=pltpu.VMEM)],
        out_specs=pl.BlockSpec((t, N_HEADS * KV_RANK), lambda i, s: (i, 0)),
        scratch_shapes=[pltpu.VMEM((N_HEADS, 1, t), F32),
                        pltpu.VMEM((N_HEADS, 1, t), F32),
                        pltpu.VMEM((N_HEADS, KV_RANK, t), F32)],
    )
    return pl.pallas_call(
        functools.partial(_prompt_attn_kernel, t=t),
        grid_spec=grid_spec,
        out_shape=jax.ShapeDtypeStruct((seq, N_HEADS * KV_RANK), BF16),
        compiler_params=pltpu.CompilerParams(
            dimension_semantics=("arbitrary",), vmem_limit_bytes=VMEM_LIMIT),
    )(safe, qcat, kcat, vt)


def _epilogue(o, sz, ga, yp, x, wts, tm):
    n_rows, d_model = x.shape
    rows = pl.BlockSpec((tm, d_model), lambda i: (i, 0))
    ws = [wts[k] for k in ("w_uv", "w_oatt", "w_out", "g_final")]
    return pl.pallas_call(
        _epilogue_kernel,
        grid=(n_rows // tm,),
        in_specs=[rows] * 5 + [_const_spec(w.shape) for w in ws],
        out_specs=rows,
        out_shape=jax.ShapeDtypeStruct((n_rows, d_model), F32),
        compiler_params=pltpu.CompilerParams(
            dimension_semantics=("arbitrary",), vmem_limit_bytes=VMEM_LIMIT),
    )(o, sz, ga, yp, x, *ws)


def _sample_proj(x, cs, hist2d, wts, past_len):
    b, d_model = x.shape
    ws = [wts[k] for k in _PROJ_WEIGHTS]
    wide = jax.ShapeDtypeStruct((b, d_model), F32)
    out_shape = (
        jax.ShapeDtypeStruct((b, N_HEADS * LANES), F32),
        jax.ShapeDtypeStruct((b, N_HEADS * LANES), F32),
        jax.ShapeDtypeStruct((b, KV_RANK), F32),
        jax.ShapeDtypeStruct((b, QK_ROPE), F32),
        wide, wide, wide,
        jax.ShapeDtypeStruct(hist2d.shape, F32),
    )
    args = (x, cs, hist2d, *ws)
    return pl.pallas_call(
        functools.partial(_sample_proj_kernel, past_len=past_len),
        grid=(1,),
        in_specs=[_const_spec(a.shape) for a in args],
        out_specs=tuple(_const_spec(s.shape) for s in out_shape),
        out_shape=out_shape,
        compiler_params=pltpu.CompilerParams(
            dimension_semantics=("arbitrary",), vmem_limit_bytes=VMEM_LIMIT),
    )(*args)


def _sample_attn(page_table, qat, qrt, knew, rnew, cache_ckv, cache_krope):
    b, n_pages = page_table.shape
    page = cache_ckv.shape[1]
    per_batch = lambda rows, width: pl.BlockSpec((1, rows, width), lambda i, pt: (i, 0, 0))
    grid_spec = pltpu.PrefetchScalarGridSpec(
        num_scalar_prefetch=1,
        grid=(b,),
        in_specs=[per_batch(KV_RANK, LANES), per_batch(QK_ROPE, LANES),
                  per_batch(1, KV_RANK), per_batch(1, QK_ROPE),
                  pl.BlockSpec(memory_space=pl.ANY),
                  pl.BlockSpec(memory_space=pl.ANY)],
        out_specs=per_batch(N_HEADS, KV_RANK),
        scratch_shapes=[pltpu.VMEM((2, n_pages // 2, 2, page, KV_RANK), F32),
                        pltpu.VMEM((2, n_pages // 2, 2 * QK_ROPE, page), F32),
                        pltpu.SemaphoreType.DMA((2, 2))],
    )
    return pl.pallas_call(
        functools.partial(_sample_attn_kernel, n_pages=n_pages),
        grid_spec=grid_spec,
        out_shape=jax.ShapeDtypeStruct((b, N_HEADS, KV_RANK), F32),
        compiler_params=pltpu.CompilerParams(
            dimension_semantics=("arbitrary",), vmem_limit_bytes=VMEM_LIMIT),
    )(page_table.reshape(-1), qat, qrt, knew, rnew, cache_ckv, jnp.swapaxes(cache_krope, 1, 2))


def kernel(x_prompt, x_sample, cache_ckv, cache_krope, state_pool, page_table, g_pre, w_in, g_q,
           w_uq, g_kv, w_uk, w_uv, w_o_att, w_pool, pool_scale, w_o_pool, w_out, g_final):
    depth = g_pre.shape[0]
    batch, seq, d_model = x_prompt.shape
    dec_b, dec_seq, _ = x_sample.shape
    assert depth == 1 and batch == 1 and dec_seq == 1
    past_len = page_table.shape[1] * cache_ckv.shape[2]
    wts = _prep_weights(g_pre[0], w_in[0], g_q[0], w_uq[0], g_kv[0], w_uk[0], w_uv[0], w_o_att[0],
                        w_pool[0], pool_scale[0], w_o_pool[0], w_out[0], g_final)

    tile = 512
    hist_rows = 16
    xp = x_prompt.reshape(seq, d_model)
    cs_p = _rope_table(jnp.arange(seq, dtype=jnp.int32))
    hist0 = jnp.zeros((hist_rows, POOL_WIDTH), F32)
    qcat, kcat, vt, ckv_p, kr_p, sz, ga, yp, pstate, slack = _prompt_proj(xp, cs_p, hist0, wts, tile)
    safe = (slack[:, 0, 0] <= SAFE_SLACK).astype(jnp.int32)
    o_p = _prompt_attn(safe, qcat, kcat, vt, tile)
    y_p = _epilogue(o_p, sz, ga, yp, xp, wts, tile)

    xs = x_sample.reshape(dec_b, d_model)
    cs_s = _rope_table(jnp.full((dec_b,), past_len, jnp.int32))
    hist2d = state_pool[0].reshape(dec_b, POOL_HIST * POOL_WIDTH)
    qa, qr, ckv_s, kr_s, sz_s, ga_s, yp_s, pstate_s = _sample_proj(xs, cs_s, hist2d, wts, past_len)
    qa3 = qa.reshape(dec_b, N_HEADS, LANES)
    qr3 = qr.reshape(dec_b, N_HEADS, LANES)[:, :, :QK_ROPE]
    qat = jnp.tile(jnp.transpose(qa3, (0, 2, 1)), (1, 1, KEYS_PER_ROW)).astype(BF16)
    qrt = jnp.tile(jnp.transpose(qr3, (0, 2, 1)), (1, 1, KEYS_PER_ROW)).astype(BF16)
    o_s = _sample_attn(page_table, qat, qrt, ckv_s.reshape(dec_b, 1, KV_RANK),
                       kr_s.reshape(dec_b, 1, QK_ROPE), cache_ckv[0], cache_krope[0])
    o_s = o_s.reshape(dec_b, N_HEADS * KV_RANK).astype(BF16)
    y_s = _epilogue(o_s, sz_s, ga_s, yp_s, xs, wts, dec_b)

    return (y_p.reshape(batch, seq, d_model),
            y_s.reshape(dec_b, 1, d_model),
            ckv_p.reshape(1, batch, seq, KV_RANK),
            kr_p.reshape(1, batch, seq, QK_ROPE),
            pstate[hist_rows - POOL_HIST:].reshape(1, batch, POOL_HIST, POOL_WIDTH),
            ckv_s.reshape(1, dec_b, 1, KV_RANK),
            kr_s.reshape(1, dec_b, 1, QK_ROPE),
            pstate_s.reshape(1, dec_b, POOL_HIST, POOL_WIDTH))
```

```python
import functools

import jax
import jax.numpy as jnp
from jax import lax
from jax.experimental import pallas as pl
from jax.experimental.pallas import tpu as pltpu

F32 = jnp.float32
BF16 = jnp.bfloat16

N_HEADS = 8
QK_NOPE = 128
QK_ROPE = 64
V_DIM = 128
Q_RANK = 256
KV_RANK = 128
ROPE_BASE = 10000.0
POOL_WINDOWS = (2, 4, 8, 16)
POOL_GROUP_DIM = 128
POOL_WIDTH = len(POOL_WINDOWS) * POOL_GROUP_DIM
POOL_HIST = 15
EPS = 1e-6
NEG_INF = -1e30
SOFTMAX_SCALE = (QK_NOPE + QK_ROPE) ** -0.5
LOG2_E = 1.4426950408889634
Q_SCALE = SOFTMAX_SCALE * LOG2_E
SAFE_SLACK = 64.0

LANES = 128
KEYS_PER_ROW = LANES // N_HEADS
QCAT = 2 * LANES
VMEM_LIMIT = 56 * 1024 * 1024


def _dot(a, b):
    return jnp.dot(a, b, preferred_element_type=F32)


def _dot_nt(a, b):
    return lax.dot_general(a, b, (((1,), (1,)), ((), ())), preferred_element_type=F32)


def _rms(x, g):
    return x * lax.rsqrt(jnp.mean(x * x, axis=-1, keepdims=True) + EPS) * g


def _silu(x):
    return x * jax.nn.sigmoid(x)


def _rope_pair(t):
    return t + pltpu.roll(t, QK_ROPE, axis=1)


def _project(x, cs, g_pre, w_a, g_q, w_qn, w_qr, w_ukt, g_kv):
    h = _rms(x, g_pre).astype(BF16)
    za = _dot(h, w_a)
    qn = _rms(za[:, :Q_RANK], g_q).astype(BF16)
    ckv = _rms(za[:, Q_RANK:Q_RANK + KV_RANK], g_kv)
    kr2 = _rope_pair(za[:, Q_RANK + KV_RANK:] * cs)
    q_nope = _dot(qn, w_qn)
    q_rope = _dot(qn, w_qr)
    qa, qr = [], []
    for hd in range(N_HEADS):
        sl = slice(hd * LANES, (hd + 1) * LANES)
        qa.append(_dot(q_nope[:, sl].astype(BF16), w_ukt[hd]) * Q_SCALE)
        qr.append(_rope_pair(q_rope[:, sl] * cs) * Q_SCALE)
    return h, qa, qr, ckv, kr2


def _pool_branch(h, win_sum, u, rcnt, w_zp, w_gp, w_pool, pool_scale, w_o_pool):
    mixed = []
    for g in range(len(POOL_WINDOWS)):
        sl = slice(g * POOL_GROUP_DIM, (g + 1) * POOL_GROUP_DIM)
        pooled = win_sum[g] * rcnt[g] - u[:, sl]
        mixed.append(_dot(pooled.astype(BF16), w_pool[g]))
    p = jnp.concatenate(mixed, axis=1) * pool_scale
    z_pool = _dot(h, w_zp)
    y_pool = _dot((p * _silu(z_pool)).astype(BF16), w_o_pool)
    return jax.nn.sigmoid(_dot(h, w_gp)) * y_pool


def _prompt_proj_kernel(x_ref, rope_a_ref, rope_b_ref, hist_ref, g_pre_ref, w_a_ref, w_za_ref, w_up_ref,
                        w_zp_ref, w_ga_ref, w_gp_ref, g_q_ref, w_qn_ref, w_qr_ref, w_ukt_ref,
                        g_kv_ref, w_pool_ref, pscale_ref, w_opool_ref,
                        qcat_ref, kcat_ref, vt_ref, ckv_ref, kr_ref, sz_ref, ga_ref, yp_ref,
                        pstate_ref, slack_ref, uext_ref, kmax_ref, *, tm):
    i = pl.program_id(0)
    hist_rows = uext_ref.shape[0] - tm

    @pl.when(i == 0)
    def _():
        uext_ref[0:hist_rows, :] = hist_ref[...]
        kmax_ref[...] = jnp.zeros(kmax_ref.shape, F32)

    cs = jnp.concatenate(
        [rope_a_ref[a, 0:1, :] * rope_b_ref[0] + rope_a_ref[a, 1:2, :] * rope_b_ref[1]
         for a in range(rope_a_ref.shape[0])], axis=0)
    h, qa, qr, ckv, kr2 = _project(x_ref[...], cs, g_pre_ref[...], w_a_ref[...], g_q_ref[...],
                                   w_qn_ref[...], w_qr_ref[...], w_ukt_ref, g_kv_ref[...])
    lane = lax.broadcasted_iota(jnp.int32, kr2.shape, 1)
    is_rope = lane < QK_ROPE
    is_shift = lane == QK_ROPE
    rsum = lambda a: jnp.sum(a, axis=1, keepdims=True)
    ckv_b = ckv.astype(BF16).astype(F32)
    kr_b = jnp.where(is_rope, kr2, 0.0).astype(BF16).astype(F32)
    k_norm2 = jnp.max(rsum(ckv_b * ckv_b + kr_b * kr_b), axis=0, keepdims=True)
    k_norm2 = jnp.maximum(k_norm2, kmax_ref[0:1, 0:1])
    kmax_ref[...] = jnp.broadcast_to(k_norm2, kmax_ref.shape)
    slack = None
    for hd in range(N_HEADS):
        qa_b = qa[hd].astype(BF16)
        qr_b = jnp.where(is_rope, qr[hd], 0.0).astype(BF16).astype(F32)
        qa_f = qa_b.astype(F32)
        c = jnp.sqrt(rsum(qa_f * qa_f + qr_b * qr_b) * k_norm2) + 1.0
        c = c.astype(BF16).astype(F32)
        own = rsum(qa_f * ckv_b + qr_b * kr_b)
        worst = jnp.max(c - own, axis=0, keepdims=True)
        slack = worst if slack is None else jnp.maximum(slack, worst)
        qcat_ref[hd, :, 0:LANES] = qa_b
        qcat_ref[hd, :, LANES:QCAT] = jnp.where(is_shift, -c, qr_b).astype(BF16)
    slack_ref[0] = jnp.broadcast_to(slack, slack_ref.shape[1:])
    kcat_ref[:, 0:LANES] = ckv.astype(BF16)
    kcat_ref[:, LANES:QCAT] = jnp.where(is_shift, 1.0, kr_b).astype(BF16)
    vt_ref[0] = ckv.T.astype(BF16)
    ckv_ref[...] = ckv
    kr_ref[...] = kr2.T[:QK_ROPE, :]

    sz_ref[...] = _silu(_dot(h, w_za_ref[...]))
    ga_ref[...] = jax.nn.sigmoid(_dot(h, w_ga_ref[...]))

    u = _dot(h, w_up_ref[...])
    uext_ref[hist_rows:, :] = u
    pos = i * tm + lax.broadcasted_iota(jnp.int32, (tm, 1), 0)
    win_sum, rcnt = [], []
    for g, w in enumerate(POOL_WINDOWS):
        c0 = g * POOL_GROUP_DIM
        acc = u[:, c0:c0 + POOL_GROUP_DIM]
        for s in range(1, w):
            acc = acc + uext_ref[hist_rows - s:hist_rows - s + tm, c0:c0 + POOL_GROUP_DIM]
        win_sum.append(acc)
        rcnt.append(1.0 / jnp.minimum(w, pos + 1).astype(F32))
    yp_ref[...] = _pool_branch(h, win_sum, u, rcnt, w_zp_ref[...], w_gp_ref[...], w_pool_ref,
                               pscale_ref[...], w_opool_ref[...])
    tail = uext_ref[tm:tm + hist_rows, :]
    pstate_ref[...] = tail
    uext_ref[0:hist_rows, :] = tail


def _prompt_attn_kernel(safe_ref, q_ref, k_ref, vt_ref, o_ref, m_sc, l_sc, acc_sc, *, t):
    i = pl.program_id(0)
    l_sc[...] = jnp.zeros(l_sc.shape, F32)
    acc_sc[...] = jnp.zeros(acc_sc.shape, F32)

    def chunk(j, masked, shifted):
        kc = k_ref[pl.ds(pl.multiple_of(j * t, t), t), :]
        vt = vt_ref[j]
        if masked:
            key = lax.broadcasted_iota(jnp.int32, (t, t), 0)
            qry = lax.broadcasted_iota(jnp.int32, (t, t), 1)
            keep = key <= qry
        st_next = _dot_nt(kc, q_ref[0])
        for hd in range(N_HEADS):
            st = st_next
            if hd + 1 < N_HEADS:
                st_next = _dot_nt(kc, q_ref[hd + 1])
            if masked:
                st = jnp.where(keep, st, NEG_INF)
            if shifted:
                p = jnp.exp2(st)
                l_sc[hd] += jnp.sum(p, axis=0, keepdims=True)
                acc_sc[hd] += _dot(vt, p.astype(BF16))
            else:
                m_prev = m_sc[hd]
                m_new = jnp.maximum(m_prev, jnp.max(st, axis=0, keepdims=True))
                alpha = jnp.exp2(m_prev - m_new)
                p = jnp.exp2(st - m_new)
                l_sc[hd] = alpha * l_sc[hd] + jnp.sum(p, axis=0, keepdims=True)
                acc_sc[hd] = alpha * acc_sc[hd] + _dot(vt, p.astype(BF16))
                m_sc[hd] = m_new

    def sweep(shifted):
        def body(j, carry):
            chunk(j, False, shifted)
            return carry
        lax.fori_loop(0, i, body, 0)
        chunk(i, True, shifted)

    safe = safe_ref[i] != 0

    @pl.when(safe)
    def _():
        sweep(True)

    @pl.when(jnp.logical_not(safe))
    def _():
        m_sc[...] = jnp.full(m_sc.shape, NEG_INF, F32)
        sweep(False)

    for hd in range(N_HEADS):
        o = acc_sc[hd] / l_sc[hd]
        o_ref[:, hd * KV_RANK:(hd + 1) * KV_RANK] = o.T.astype(BF16)


def _epilogue_kernel(o_ref, sz_ref, ga_ref, yp_ref, x_ref, w_uv_ref, w_oatt_ref, w_out_ref,
                     g_final_ref, y_ref):
    o = o_ref[...]
    o2 = jnp.concatenate(
        [_dot(o[:, hd * KV_RANK:(hd + 1) * KV_RANK], w_uv_ref[hd]) for hd in range(N_HEADS)], axis=1)
    y_att = _dot((o2 * sz_ref[...]).astype(BF16), w_oatt_ref[...])
    merged = ga_ref[...] * y_att + yp_ref[...]
    xo = x_ref[...] + _dot(merged.astype(BF16), w_out_ref[...])
    y_ref[...] = _rms(xo, g_final_ref[...])


def _sample_proj_kernel(x_ref, cs_ref, hist_ref, g_pre_ref, w_a_ref, w_za_ref, w_up_ref,
                        w_zp_ref, w_ga_ref, w_gp_ref, g_q_ref, w_qn_ref, w_qr_ref, w_ukt_ref,
                        g_kv_ref, w_pool_ref, pscale_ref, w_opool_ref,
                        qa_ref, qr_ref, ckv_ref, kr_ref, sz_ref, ga_ref, yp_ref, pstate_ref,
                        *, past_len):
    cs = cs_ref[...]
    h, qa, qr, ckv, kr2 = _project(x_ref[...], cs, g_pre_ref[...], w_a_ref[...], g_q_ref[...],
                                   w_qn_ref[...], w_qr_ref[...], w_ukt_ref, g_kv_ref[...])
    for hd in range(N_HEADS):
        qa_ref[:, hd * LANES:(hd + 1) * LANES] = qa[hd]
        qr_ref[:, hd * LANES:(hd + 1) * LANES] = qr[hd]
    ckv_ref[...] = ckv
    kr_ref[...] = kr2[:, :QK_ROPE]
    sz_ref[...] = _silu(_dot(h, w_za_ref[...]))
    ga_ref[...] = jax.nn.sigmoid(_dot(h, w_ga_ref[...]))

    u = _dot(h, w_up_ref[...])
    win_sum, rcnt = [], []
    for g, w in enumerate(POOL_WINDOWS):
        c0 = g * POOL_GROUP_DIM
        acc = u[:, c0:c0 + POOL_GROUP_DIM]
        for s in range(1, w):
            acc = acc + hist_ref[POOL_HIST - s, :, c0:c0 + POOL_GROUP_DIM]
        win_sum.append(acc)
        rcnt.append(1.0 / float(min(w, past_len + 1)))
    yp_ref[...] = _pool_branch(h, win_sum, u, rcnt, w_zp_ref[...], w_gp_ref[...], w_pool_ref,
                               pscale_ref[...], w_opool_ref[...])
    pstate_ref[0:POOL_HIST - 1] = hist_ref[1:POOL_HIST]
    pstate_ref[POOL_HIST - 1] = u


def _lane_group_reduce(v, op):
    sh = N_HEADS
    while sh < LANES:
        v = op(v, pltpu.roll(v, sh, axis=1))
        sh *= 2
    return v


def _diag_to_col(v):
    sub = lax.broadcasted_iota(jnp.int32, v.shape, 0)
    lane = lax.broadcasted_iota(jnp.int32, v.shape, 1)
    return jnp.sum(jnp.where(sub == lane, v, 0.0), axis=1, keepdims=True)


def _latent_rows(kbuf, slot):
    n_pairs, _, page, width = kbuf.shape[1:]
    halves = []
    for half in range(2):
        cols = []
        for jj in range(KEYS_PER_ROW // 2):
            r0 = (half * (KEYS_PER_ROW // 2) + jj) * 8
            for par in range(2):
                cols.append(kbuf[slot, :, par, r0:r0 + 8, :].reshape(n_pairs * 8, width))
        halves.append(jnp.concatenate(cols, axis=1))
    return jnp.concatenate(halves, axis=0).astype(BF16)


def _rotary_rows(rbuf, slot):
    n_pairs = rbuf.shape[1]
    xt = jnp.swapaxes(rbuf[slot], 1, 2)
    halves = []
    for half in range(2):
        cols = []
        for jj in range(KEYS_PER_ROW // 2):
            r0 = (half * (KEYS_PER_ROW // 2) + jj) * 8
            cols.append(xt[:, r0:r0 + 8, :].reshape(n_pairs * 8, LANES))
        halves.append(jnp.concatenate(cols, axis=1))
    return jnp.concatenate(halves, axis=0).astype(BF16)


def _sample_attn_kernel(pt_ref, qat_ref, qrt_ref, knew_ref, rnew_ref, ck_hbm, krt_hbm, o_ref,
                        kbuf, rbuf, sems, *, n_pages):
    b = pl.program_id(0)
    nb = pl.num_programs(0)
    slot = b % 2

    def copies(bb, sl, q):
        out = []
        for par in range(2):
            pid = pt_ref[bb * n_pages + 2 * q + par]
            out.append(pltpu.make_async_copy(ck_hbm.at[pid], kbuf.at[sl, q, par], sems.at[0, sl]))
            out.append(pltpu.make_async_copy(
                krt_hbm.at[pid], rbuf.at[sl, q, par * QK_ROPE:(par + 1) * QK_ROPE, :],
                sems.at[1, sl]))
        return out

    def start_all(bb, sl):
        def body(q, c):
            for cp in copies(bb, sl, q):
                cp.start()
            return c
        lax.fori_loop(0, n_pages // 2, body, 0)

    def wait_all(sl):
        pltpu.make_async_copy(kbuf.at[sl], kbuf.at[sl], sems.at[0, sl]).wait()
        pltpu.make_async_copy(rbuf.at[sl], rbuf.at[sl], sems.at[1, sl]).wait()

    @pl.when(b == 0)
    def _():
        start_all(b, slot)

    @pl.when(b + 1 < nb)
    def _():
        start_all(b + 1, 1 - slot)

    wait_all(slot)

    qat = qat_ref[0]
    qrt = qrt_ref[0]
    lane_a = lax.broadcasted_iota(jnp.int32, qat.shape, 1) // N_HEADS
    lane_r = lax.broadcasted_iota(jnp.int32, qrt.shape, 1) // N_HEADS
    zero = jnp.zeros((), BF16)
    wk = jnp.concatenate([jnp.where(lane_a == j, qat, zero) for j in range(KEYS_PER_ROW)], axis=0)
    wr = jnp.concatenate([jnp.where(lane_r == j, qrt, zero) for j in range(KEYS_PER_ROW)], axis=0)

    kb = _latent_rows(kbuf, slot)
    rb = _rotary_rows(rbuf, slot)
    hr = kb.shape[0] // 2
    s_lat = [_dot(kb[:hr], wk), _dot(kb[hr:], wk)]
    s_rot = [_dot(rb[:hr], wr), _dot(rb[hr:], wr)]
    s2 = jnp.concatenate([s_lat[0] + s_rot[0], s_lat[1] + s_rot[1]], axis=0)

    knew = knew_ref[0].astype(BF16)
    rnew = rnew_ref[0].astype(BF16)
    s_new = (_dot(jnp.broadcast_to(knew, (8, KV_RANK)), qat)
             + _dot(jnp.broadcast_to(rnew, (8, QK_ROPE)), qrt))

    m = jnp.max(s2, axis=0, keepdims=True)
    m = _lane_group_reduce(jnp.broadcast_to(m, (8, LANES)), jnp.maximum)
    m = jnp.maximum(m, s_new)
    p2 = jnp.exp2(s2 - m[0:1])
    p_new = jnp.exp2(s_new - m)
    l_lane = _lane_group_reduce(
        jnp.broadcast_to(jnp.sum(p2, axis=0, keepdims=True), (8, LANES)), jnp.add)
    p_new_col = _diag_to_col(p_new)
    l_col = _diag_to_col(l_lane) + p_new_col

    p2t = p2.T.astype(BF16)
    hs = KEYS_PER_ROW // 2
    hp, hk = hs * N_HEADS, hs * KV_RANK
    g_halves = [_dot(p2t[:hp], kb[:, :hk]), _dot(p2t[hp:], kb[:, hk:])]
    o = p_new_col.astype(BF16).astype(F32) * knew.astype(F32)
    for g in g_halves:
        for j in range(hs):
            o = o + g[j * N_HEADS:(j + 1) * N_HEADS, j * KV_RANK:(j + 1) * KV_RANK]
    o_ref[0] = o / l_col


def _const_spec(shape):
    zeros = (0,) * len(shape)
    return pl.BlockSpec(shape, lambda *_: zeros)


def _prep_weights(g_pre, w_in, g_q, w_uq, g_kv, w_uk, w_uv, w_o_att, w_pool, pool_scale,
                  w_o_pool, w_out, g_final):
    d_model = w_in.shape[0]
    att_w = N_HEADS * V_DIM
    half = QK_ROPE // 2
    o_kv = Q_RANK
    o_kr = o_kv + KV_RANK
    o_za = o_kr + QK_ROPE
    o_up = o_za + att_w
    o_zp = o_up + POOL_WIDTH
    o_ga = o_zp + POOL_WIDTH
    o_gp = o_ga + d_model
    w_kr = w_in[:, o_kr:o_za]
    w_a = jnp.concatenate([w_in[:, :o_kr], w_kr, -w_kr[:, half:], w_kr[:, :half]], axis=1)
    uq = w_uq.reshape(Q_RANK, N_HEADS, QK_NOPE + QK_ROPE)
    w_qn = uq[:, :, :QK_NOPE].reshape(Q_RANK, N_HEADS * QK_NOPE)
    x1, x2 = uq[:, :, QK_NOPE:QK_NOPE + half], uq[:, :, QK_NOPE + half:]
    w_qr = jnp.concatenate([x1, x2, -x2, x1], axis=2).reshape(Q_RANK, N_HEADS * LANES)
    bf = lambda a: a.astype(BF16)
    row = lambda a: a.reshape(1, -1).astype(F32)
    return dict(
        g_pre=row(g_pre), w_a=bf(w_a), w_za=bf(w_in[:, o_za:o_up]), w_up=bf(w_in[:, o_up:o_zp]),
        w_zp=bf(w_in[:, o_zp:o_ga]), w_ga=bf(w_in[:, o_ga:o_gp]), w_gp=bf(w_in[:, o_gp:]),
        g_q=row(g_q), w_qn=bf(w_qn), w_qr=bf(w_qr),
        w_ukt=bf(jnp.transpose(w_uk, (1, 2, 0))),
        g_kv=row(g_kv), w_pool=bf(w_pool), pscale=row(pool_scale), w_opool=bf(w_o_pool),
        w_uv=bf(jnp.transpose(w_uv, (1, 0, 2))),
        w_oatt=bf(w_o_att), w_out=bf(w_out), g_final=row(g_final))


_PROJ_WEIGHTS = ("g_pre", "w_a", "w_za", "w_up", "w_zp", "w_ga", "w_gp", "g_q", "w_qn", "w_qr",
                 "w_ukt", "g_kv", "w_pool", "pscale", "w_opool")


def _cos_sin(pos):
    half = QK_ROPE // 2
    inv = ROPE_BASE ** (-jnp.arange(half, dtype=F32) / half)
    ang = pos.astype(F32)[:, None] * inv[None, :]
    return jnp.cos(ang), jnp.sin(ang)


def _rope_table(pos):
    c, s = _cos_sin(pos)
    return jnp.concatenate([c, c, s, s], axis=1)


ROPE_BLOCK = 128


def _rope_factors(seq):
    ca, sa = _cos_sin(jnp.arange(0, seq, ROPE_BLOCK, dtype=jnp.int32))
    cb, sb = _cos_sin(jnp.arange(ROPE_BLOCK, dtype=jnp.int32))
    rope_a = jnp.stack([jnp.concatenate([ca, ca, sa, sa], axis=1),
                        jnp.concatenate([-sa, -sa, ca, ca], axis=1)], axis=1)
    rope_b = jnp.stack([jnp.concatenate([cb] * 4, axis=1),
                        jnp.concatenate([sb] * 4, axis=1)], axis=0)
    return rope_a, rope_b


def _prompt_proj(x, rope_a, rope_b, hist, wts, tm):
    seq, d_model = x.shape
    n = seq // tm
    blocks = tm // ROPE_BLOCK
    hist_rows = hist.shape[0]
    ws = [wts[k] for k in _PROJ_WEIGHTS]
    rows = lambda width: pl.BlockSpec((tm, width), lambda i: (i, 0))
    out_shape = (
        jax.ShapeDtypeStruct((N_HEADS, seq, QCAT), BF16),
        jax.ShapeDtypeStruct((seq, QCAT), BF16),
        jax.ShapeDtypeStruct((n, KV_RANK, tm), BF16),
        jax.ShapeDtypeStruct((seq, KV_RANK), F32),
        jax.ShapeDtypeStruct((QK_ROPE, seq), F32),
        jax.ShapeDtypeStruct((seq, d_model), F32),
        jax.ShapeDtypeStruct((seq, d_model), F32),
        jax.ShapeDtypeStruct((seq, d_model), F32),
        jax.ShapeDtypeStruct((hist_rows, POOL_WIDTH), F32),
        jax.ShapeDtypeStruct((n, 8, LANES), F32),
    )
    out_specs = (
        pl.BlockSpec((N_HEADS, tm, QCAT), lambda i: (0, i, 0)),
        rows(QCAT),
        pl.BlockSpec((1, KV_RANK, tm), lambda i: (i, 0, 0)),
        rows(KV_RANK), pl.BlockSpec((QK_ROPE, tm), lambda i: (0, i)),
        rows(d_model), rows(d_model), rows(d_model),
        _const_spec((hist_rows, POOL_WIDTH)),
        pl.BlockSpec((1, 8, LANES), lambda i: (i, 0, 0)),
    )
    return pl.pallas_call(
        functools.partial(_prompt_proj_kernel, tm=tm),
        grid=(n,),
        in_specs=[rows(d_model), pl.BlockSpec((blocks, 2, LANES), lambda i: (i, 0, 0)),
                  _const_spec(rope_b.shape), _const_spec(hist.shape)]
        + [_const_spec(w.shape) for w in ws],
        out_specs=out_specs,
        out_shape=out_shape,
        scratch_shapes=[pltpu.VMEM((hist_rows + tm, POOL_WIDTH), F32),
                        pltpu.VMEM((8, LANES), F32)],
        compiler_params=pltpu.CompilerParams(
            dimension_semantics=("arbitrary",), vmem_limit_bytes=VMEM_LIMIT),
    )(x, rope_a, rope_b, hist, *ws)


def _prompt_attn(safe, qcat, kcat, vt, t):
    seq = kcat.shape[0]
    n = seq // t
    grid_spec = pltpu.PrefetchScalarGridSpec(
        num_scalar_prefetch=1,
        grid=(n,),
        in_specs=[pl.BlockSpec((N_HEADS, t, QCAT), lambda i, s: (0, i, 0)),
                  pl.BlockSpec(memory_space=pltpu.VMEM),
                  pl.BlockSpec(memory_space=pltpu.VMEM)],
        out_specs=pl.BlockSpec((t, N_HEADS * KV_RANK), lambda i, s: (i, 0)),
        scratch_shapes=[pltpu.VMEM((N_HEADS, 1, t), F32),
                        pltpu.VMEM((N_HEADS, 1, t), F32),
                        pltpu.VMEM((N_HEADS, KV_RANK, t), F32)],
    )
    return pl.pallas_call(
        functools.partial(_prompt_attn_kernel, t=t),
        grid_spec=grid_spec,
        out_shape=jax.ShapeDtypeStruct((seq, N_HEADS * KV_RANK), BF16),
        compiler_params=pltpu.CompilerParams(
            dimension_semantics=("arbitrary",), vmem_limit_bytes=VMEM_LIMIT),
    )(safe, qcat, kcat, vt)


def _epilogue(o, sz, ga, yp, x, wts, tm):
    n_rows, d_model = x.shape
    rows = pl.BlockSpec((tm, d_model), lambda i: (i, 0))
    ws = [wts[k] for k in ("w_uv", "w_oatt", "w_out", "g_final")]
    return pl.pallas_call(
        _epilogue_kernel,
        grid=(n_rows // tm,),
        in_specs=[rows] * 5 + [_const_spec(w.shape) for w in ws],
        out_specs=rows,
        out_shape=jax.ShapeDtypeStruct((n_rows, d_model), F32),
        compiler_params=pltpu.CompilerParams(
            dimension_semantics=("arbitrary",), vmem_limit_bytes=VMEM_LIMIT),
    )(o, sz, ga, yp, x, *ws)


def _sample_proj(x, cs, hist, wts, past_len):
    b, d_model = x.shape
    ws = [wts[k] for k in _PROJ_WEIGHTS]
    wide = jax.ShapeDtypeStruct((b, d_model), F32)
    out_shape = (
        jax.ShapeDtypeStruct((b, N_HEADS * LANES), F32),
        jax.ShapeDtypeStruct((b, N_HEADS * LANES), F32),
        jax.ShapeDtypeStruct((b, KV_RANK), F32),
        jax.ShapeDtypeStruct((b, QK_ROPE), F32),
        wide, wide, wide,
        jax.ShapeDtypeStruct(hist.shape, F32),
    )
    args = (x, cs, hist, *ws)
    return pl.pallas_call(
        functools.partial(_sample_proj_kernel, past_len=past_len),
        grid=(1,),
        in_specs=[_const_spec(a.shape) for a in args],
        out_specs=tuple(_const_spec(s.shape) for s in out_shape),
        out_shape=out_shape,
        compiler_params=pltpu.CompilerParams(
            dimension_semantics=("arbitrary",), vmem_limit_bytes=VMEM_LIMIT),
    )(*args)


def _sample_attn(page_table, qat, qrt, knew, rnew, cache_ckv, cache_krope):
    b, n_pages = page_table.shape
    page = cache_ckv.shape[1]
    per_batch = lambda rows, width: pl.BlockSpec((1, rows, width), lambda i, pt: (i, 0, 0))
    grid_spec = pltpu.PrefetchScalarGridSpec(
        num_scalar_prefetch=1,
        grid=(b,),
        in_specs=[per_batch(KV_RANK, LANES), per_batch(QK_ROPE, LANES),
                  per_batch(1, KV_RANK), per_batch(1, QK_ROPE),
                  pl.BlockSpec(memory_space=pl.ANY),
                  pl.BlockSpec(memory_space=pl.ANY)],
        out_specs=per_batch(N_HEADS, KV_RANK),
        scratch_shapes=[pltpu.VMEM((2, n_pages // 2, 2, page, KV_RANK), F32),
                        pltpu.VMEM((2, n_pages // 2, 2 * QK_ROPE, page), F32),
                        pltpu.SemaphoreType.DMA((2, 2))],
    )
    return pl.pallas_call(
        functools.partial(_sample_attn_kernel, n_pages=n_pages),
        grid_spec=grid_spec,
        out_shape=jax.ShapeDtypeStruct((b, N_HEADS, KV_RANK), F32),
        compiler_params=pltpu.CompilerParams(
            dimension_semantics=("arbitrary",), vmem_limit_bytes=VMEM_LIMIT),
    )(page_table.reshape(-1), qat, qrt, knew, rnew, cache_ckv, jnp.swapaxes(cache_krope, 1, 2))


def kernel(x_prompt, x_sample, cache_ckv, cache_krope, state_pool, page_table, g_pre, w_in, g_q,
           w_uq, g_kv, w_uk, w_uv, w_o_att, w_pool, pool_scale, w_o_pool, w_out, g_final):
    depth = g_pre.shape[0]
    batch, seq, d_model = x_prompt.shape
    dec_b, dec_seq, _ = x_sample.shape
    assert depth == 1 and batch == 1 and dec_seq == 1
    past_len = page_table.shape[1] * cache_ckv.shape[2]
    wts = _prep_weights(g_pre[0], w_in[0], g_q[0], w_uq[0], g_kv[0], w_uk[0], w_uv[0], w_o_att[0],
                        w_pool[0], pool_scale[0], w_o_pool[0], w_out[0], g_final)

    tile = 512
    hist_rows = 16
    xp = x_prompt.reshape(seq, d_model)
    rope_a, rope_b = _rope_factors(seq)
    hist0 = jnp.zeros((hist_rows, POOL_WIDTH), F32)
    qcat, kcat, vt, ckv_p, krt_p, sz, ga, yp, pstate, slack = _prompt_proj(
        xp, rope_a, rope_b, hist0, wts, tile)
    safe = (slack[:, 0, 0] <= SAFE_SLACK).astype(jnp.int32)
    o_p = _prompt_attn(safe, qcat, kcat, vt, tile)
    y_p = _epilogue(o_p, sz, ga, yp, xp, wts, tile)

    xs = x_sample.reshape(dec_b, d_model)
    cs_s = _rope_table(jnp.full((dec_b,), past_len, jnp.int32))
    hist_s = jnp.transpose(state_pool[0], (1, 0, 2))
    qa, qr, ckv_s, kr_s, sz_s, ga_s, yp_s, pstate_s = _sample_proj(xs, cs_s, hist_s, wts, past_len)
    qa3 = qa.reshape(dec_b, N_HEADS, LANES)
    qr3 = qr.reshape(dec_b, N_HEADS, LANES)[:, :, :QK_ROPE]
    qat = jnp.tile(jnp.transpose(qa3, (0, 2, 1)), (1, 1, KEYS_PER_ROW)).astype(BF16)
    qrt = jnp.tile(jnp.transpose(qr3, (0, 2, 1)), (1, 1, KEYS_PER_ROW)).astype(BF16)
    o_s = _sample_attn(page_table, qat, qrt, ckv_s.reshape(dec_b, 1, KV_RANK),
                       kr_s.reshape(dec_b, 1, QK_ROPE), cache_ckv[0], cache_krope[0])
    o_s = o_s.reshape(dec_b, N_HEADS * KV_RANK).astype(BF16)
    y_s = _epilogue(o_s, sz_s, ga_s, yp_s, xs, wts, dec_b)

    return (y_p.reshape(batch, seq, d_model),
            y_s.reshape(dec_b, 1, d_model),
            ckv_p.reshape(1, batch, seq, KV_RANK),
            krt_p.T.reshape(1, batch, seq, QK_ROPE),
            pstate[hist_rows - POOL_HIST:].reshape(1, batch, POOL_HIST, POOL_WIDTH),
            ckv_s.reshape(1, dec_b, 1, KV_RANK),
            kr_s.reshape(1, dec_b, 1, QK_ROPE),
            jnp.transpose(pstate_s, (1, 0, 2)).reshape(1, dec_b, POOL_HIST, POOL_WIDTH))
```

```python
import functools

import jax
import jax.numpy as jnp
from jax import lax
from jax.experimental import pallas as pl
from jax.experimental.pallas import tpu as pltpu

F32 = jnp.float32
BF16 = jnp.bfloat16

N_HEADS = 8
QK_NOPE = 128
QK_ROPE = 64
V_DIM = 128
Q_RANK = 256
KV_RANK = 128
ROPE_BASE = 10000.0
POOL_WINDOWS = (2, 4, 8, 16)
POOL_GROUP_DIM = 128
POOL_WIDTH = len(POOL_WINDOWS) * POOL_GROUP_DIM
POOL_HIST = 15
EPS = 1e-6
NEG_INF = -1e30
SOFTMAX_SCALE = (QK_NOPE + QK_ROPE) ** -0.5
LOG2_E = 1.4426950408889634
Q_SCALE = SOFTMAX_SCALE * LOG2_E
SAFE_SLACK = 64.0

LANES = 128
KEYS_PER_ROW = LANES // N_HEADS
QCAT = 2 * LANES
VMEM_LIMIT = 56 * 1024 * 1024


def _dot(a, b):
    return jnp.dot(a, b, preferred_element_type=F32)


def _dot_nt(a, b):
    return lax.dot_general(a, b, (((1,), (1,)), ((), ())), preferred_element_type=F32)


def _rms(x, g):
    return x * lax.rsqrt(jnp.mean(x * x, axis=-1, keepdims=True) + EPS) * g


def _silu(x):
    return x * jax.nn.sigmoid(x)


def _rope_pair(t):
    return t + pltpu.roll(t, QK_ROPE, axis=1)


def _project(x, cs, g_pre, w_a, g_q, w_qn, w_qr, w_ukt, g_kv):
    h = _rms(x, g_pre).astype(BF16)
    za = _dot_nt(h, w_a)
    qn = _rms(za[:, :Q_RANK], g_q).astype(BF16)
    ckv = _rms(za[:, Q_RANK:Q_RANK + KV_RANK], g_kv)
    kr2 = _rope_pair(za[:, Q_RANK + KV_RANK:] * cs)
    q_nope = _dot(qn, w_qn)
    q_rope = _dot(qn, w_qr)
    qa, qr = [], []
    for hd in range(N_HEADS):
        sl = slice(hd * LANES, (hd + 1) * LANES)
        qa.append(_dot(q_nope[:, sl].astype(BF16), w_ukt[hd]) * Q_SCALE)
        qr.append(_rope_pair(q_rope[:, sl] * cs) * Q_SCALE)
    return h, qa, qr, ckv, kr2


def _pool_branch(h, win_sum, u, rcnt, w_zp, w_gp, w_pool, pool_scale, w_o_pool):
    mixed = []
    for g in range(len(POOL_WINDOWS)):
        sl = slice(g * POOL_GROUP_DIM, (g + 1) * POOL_GROUP_DIM)
        pooled = win_sum[g] * rcnt[g] - u[:, sl]
        mixed.append(_dot(pooled.astype(BF16), w_pool[g]))
    p = jnp.concatenate(mixed, axis=1) * pool_scale
    z_pool = _dot_nt(h, w_zp)
    y_pool = _dot((p * _silu(z_pool)).astype(BF16), w_o_pool)
    return jax.nn.sigmoid(_dot_nt(h, w_gp)) * y_pool


def _prompt_proj_kernel(x_ref, rope_a_ref, rope_b_ref, hist_ref, g_pre_ref, w_a_ref, w_za_ref, w_up_ref,
                        w_zp_ref, w_ga_ref, w_gp_ref, g_q_ref, w_qn_ref, w_qr_ref, w_ukt_ref,
                        g_kv_ref, w_pool_ref, pscale_ref, w_opool_ref,
                        qcat_ref, kcat_ref, vt_ref, ckv_ref, kr_ref, sz_ref, ga_ref, yp_ref,
                        pstate_ref, slack_ref, uext_ref, kmax_ref, *, tm):
    i = pl.program_id(0)
    hist_rows = uext_ref.shape[0] - tm

    @pl.when(i == 0)
    def _():
        uext_ref[0:hist_rows, :] = hist_ref[...]
        kmax_ref[...] = jnp.zeros(kmax_ref.shape, F32)

    cs = jnp.concatenate(
        [rope_a_ref[a, 0:1, :] * rope_b_ref[0] + rope_a_ref[a, 1:2, :] * rope_b_ref[1]
         for a in range(rope_a_ref.shape[0])], axis=0)
    h, qa, qr, ckv, kr2 = _project(x_ref[...], cs, g_pre_ref[...], w_a_ref[...], g_q_ref[...],
                                   w_qn_ref[...], w_qr_ref[...], w_ukt_ref, g_kv_ref[...])
    lane = lax.broadcasted_iota(jnp.int32, kr2.shape, 1)
    is_rope = lane < QK_ROPE
    is_shift = lane == QK_ROPE
    rsum = lambda a: jnp.sum(a, axis=1, keepdims=True)
    ckv_b = ckv.astype(BF16).astype(F32)
    kr_b = jnp.where(is_rope, kr2, 0.0).astype(BF16).astype(F32)
    k_norm2 = jnp.max(rsum(ckv_b * ckv_b + kr_b * kr_b), axis=0, keepdims=True)
    k_norm2 = jnp.maximum(k_norm2, kmax_ref[0:1, 0:1])
    kmax_ref[...] = jnp.broadcast_to(k_norm2, kmax_ref.shape)
    slack = None
    for hd in range(N_HEADS):
        qa_b = qa[hd].astype(BF16)
        qr_b = jnp.where(is_rope, qr[hd], 0.0).astype(BF16).astype(F32)
        qa_f = qa_b.astype(F32)
        c = jnp.sqrt(rsum(qa_f * qa_f + qr_b * qr_b) * k_norm2) + 1.0
        c = c.astype(BF16).astype(F32)
        own = rsum(qa_f * ckv_b + qr_b * kr_b)
        worst = jnp.max(c - own, axis=0, keepdims=True)
        slack = worst if slack is None else jnp.maximum(slack, worst)
        qcat_ref[hd, :, 0:LANES] = qa_b
        qcat_ref[hd, :, LANES:QCAT] = jnp.where(is_shift, -c, qr_b).astype(BF16)
    slack_ref[0] = jnp.broadcast_to(slack, slack_ref.shape[1:])
    kcat_ref[:, 0:LANES] = ckv.astype(BF16)
    kcat_ref[:, LANES:QCAT] = jnp.where(is_shift, 1.0, kr_b).astype(BF16)
    vt_ref[0] = ckv.T.astype(BF16)
    ckv_ref[...] = ckv
    kr_ref[...] = kr2.T[:QK_ROPE, :]

    sz_ref[...] = _silu(_dot_nt(h, w_za_ref[...])).astype(sz_ref.dtype)
    ga_ref[...] = jax.nn.sigmoid(_dot_nt(h, w_ga_ref[...])).astype(ga_ref.dtype)

    u = _dot_nt(h, w_up_ref[...])
    uext_ref[hist_rows:, :] = u
    pos = i * tm + lax.broadcasted_iota(jnp.int32, (tm, 1), 0)
    win_sum, rcnt = [], []
    for g, w in enumerate(POOL_WINDOWS):
        c0 = g * POOL_GROUP_DIM
        acc = u[:, c0:c0 + POOL_GROUP_DIM]
        for s in range(1, w):
            acc = acc + uext_ref[hist_rows - s:hist_rows - s + tm, c0:c0 + POOL_GROUP_DIM]
        win_sum.append(acc)
        rcnt.append(1.0 / jnp.minimum(w, pos + 1).astype(F32))
    yp_ref[...] = _pool_branch(h, win_sum, u, rcnt, w_zp_ref[...], w_gp_ref[...], w_pool_ref,
                               pscale_ref[...], w_opool_ref[...]).astype(yp_ref.dtype)
    tail = uext_ref[tm:tm + hist_rows, :]
    pstate_ref[...] = tail
    uext_ref[0:hist_rows, :] = tail


def _prompt_attn_kernel(safe_ref, q_ref, k_ref, vt_ref, o_ref, m_sc, l_sc, acc_sc, *, t):
    i = pl.program_id(0)
    l_sc[...] = jnp.zeros(l_sc.shape, F32)
    acc_sc[...] = jnp.zeros(acc_sc.shape, F32)

    def chunks(specs, shifted):
        kcs = [k_ref[pl.ds(pl.multiple_of(j * t, t), t), :] for j, _ in specs]
        vts = [vt_ref[j] for j, _ in specs]
        units = [(c, hd) for c in range(len(specs)) for hd in range(N_HEADS)]
        if any(masked for _, masked in specs):
            key = lax.broadcasted_iota(jnp.int32, (t, t), 0)
            qry = lax.broadcasted_iota(jnp.int32, (t, t), 1)
            keep = key <= qry
        st_next = _dot_nt(kcs[0], q_ref[0])
        for u, (c, hd) in enumerate(units):
            st = st_next
            if u + 1 < len(units):
                cn, hn = units[u + 1]
                st_next = _dot_nt(kcs[cn], q_ref[hn])
            if specs[c][1]:
                st = jnp.where(keep, st, NEG_INF)
            if shifted:
                p = jnp.exp2(st)
                l_sc[hd] += jnp.sum(p, axis=0, keepdims=True)
                acc_sc[hd] += _dot(vts[c], p.astype(BF16))
            else:
                m_prev = m_sc[hd]
                m_new = jnp.maximum(m_prev, jnp.max(st, axis=0, keepdims=True))
                alpha = jnp.exp2(m_prev - m_new)
                p = jnp.exp2(st - m_new)
                l_sc[hd] = alpha * l_sc[hd] + jnp.sum(p, axis=0, keepdims=True)
                acc_sc[hd] = alpha * acc_sc[hd] + _dot(vts[c], p.astype(BF16))
                m_sc[hd] = m_new

    def sweep(shifted, width):
        def body(jw, carry):
            chunks([(jw * width + c, False) for c in range(width)], shifted)
            return carry
        lax.fori_loop(0, i // width, body, 0)
        if width == 1:
            chunks([(i, True)], shifted)
        else:
            assert width == 2
            odd = i % 2 == 1

            @pl.when(odd)
            def _():
                chunks([(i - 1, False), (i, True)], shifted)

            @pl.when(jnp.logical_not(odd))
            def _():
                chunks([(i, True)], shifted)

    safe = safe_ref[i] != 0

    @pl.when(safe)
    def _():
        sweep(True, 2)

    @pl.when(jnp.logical_not(safe))
    def _():
        m_sc[...] = jnp.full(m_sc.shape, NEG_INF, F32)
        sweep(False, 1)

    for hd in range(N_HEADS):
        o = acc_sc[hd] / l_sc[hd]
        o_ref[:, hd * KV_RANK:(hd + 1) * KV_RANK] = o.T.astype(BF16)


def _epilogue_kernel(o_ref, sz_ref, ga_ref, yp_ref, x_ref, w_uv_ref, w_oatt_ref, w_out_ref,
                     g_final_ref, y_ref):
    o = o_ref[...]
    o2 = jnp.concatenate(
        [_dot(o[:, hd * KV_RANK:(hd + 1) * KV_RANK], w_uv_ref[hd]) for hd in range(N_HEADS)], axis=1)
    y_att = _dot((o2 * sz_ref[...].astype(F32)).astype(BF16), w_oatt_ref[...])
    merged = ga_ref[...].astype(F32) * y_att + yp_ref[...].astype(F32)
    xo = x_ref[...] + _dot(merged.astype(BF16), w_out_ref[...])
    y_ref[...] = _rms(xo, g_final_ref[...])


def _sample_proj_kernel(x_ref, cs_ref, hist_ref, g_pre_ref, w_a_ref, w_za_ref, w_up_ref,
                        w_zp_ref, w_ga_ref, w_gp_ref, g_q_ref, w_qn_ref, w_qr_ref, w_ukt_ref,
                        g_kv_ref, w_pool_ref, pscale_ref, w_opool_ref,
                        qa_ref, qr_ref, ckv_ref, kr_ref, sz_ref, ga_ref, yp_ref, pstate_ref,
                        *, past_len):
    cs = cs_ref[...]
    h, qa, qr, ckv, kr2 = _project(x_ref[...], cs, g_pre_ref[...], w_a_ref[...], g_q_ref[...],
                                   w_qn_ref[...], w_qr_ref[...], w_ukt_ref, g_kv_ref[...])
    for hd in range(N_HEADS):
        qa_ref[:, hd * LANES:(hd + 1) * LANES] = qa[hd]
        qr_ref[:, hd * LANES:(hd + 1) * LANES] = qr[hd]
    ckv_ref[...] = ckv
    kr_ref[...] = kr2[:, :QK_ROPE]
    sz_ref[...] = _silu(_dot_nt(h, w_za_ref[...])).astype(sz_ref.dtype)
    ga_ref[...] = jax.nn.sigmoid(_dot_nt(h, w_ga_ref[...])).astype(ga_ref.dtype)

    u = _dot_nt(h, w_up_ref[...])
    win_sum, rcnt = [], []
    for g, w in enumerate(POOL_WINDOWS):
        c0 = g * POOL_GROUP_DIM
        acc = u[:, c0:c0 + POOL_GROUP_DIM]
        for s in range(1, w):
            acc = acc + hist_ref[POOL_HIST - s, :, c0:c0 + POOL_GROUP_DIM]
        win_sum.append(acc)
        rcnt.append(1.0 / float(min(w, past_len + 1)))
    yp_ref[...] = _pool_branch(h, win_sum, u, rcnt, w_zp_ref[...], w_gp_ref[...], w_pool_ref,
                               pscale_ref[...], w_opool_ref[...]).astype(yp_ref.dtype)
    pstate_ref[0:POOL_HIST - 1] = hist_ref[1:POOL_HIST]
    pstate_ref[POOL_HIST - 1] = u


def _lane_group_reduce(v, op):
    sh = N_HEADS
    while sh < LANES:
        v = op(v, pltpu.roll(v, sh, axis=1))
        sh *= 2
    return v


def _diag_to_col(v):
    sub = lax.broadcasted_iota(jnp.int32, v.shape, 0)
    lane = lax.broadcasted_iota(jnp.int32, v.shape, 1)
    return jnp.sum(jnp.where(sub == lane, v, 0.0), axis=1, keepdims=True)


def _latent_rows(kbuf):
    n_pairs, _, page, width = kbuf.shape
    halves = []
    for half in range(2):
        cols = []
        for jj in range(KEYS_PER_ROW // 2):
            r0 = (half * (KEYS_PER_ROW // 2) + jj) * 8
            for par in range(2):
                cols.append(kbuf[:, par, r0:r0 + 8, :].reshape(n_pairs * 8, width))
        halves.append(jnp.concatenate(cols, axis=1))
    return jnp.concatenate(halves, axis=0).astype(BF16)


def _rotary_rows(rbuf):
    n_pairs = rbuf.shape[0]
    xt = jnp.swapaxes(rbuf[...], 1, 2)
    halves = []
    for half in range(2):
        cols = []
        for jj in range(KEYS_PER_ROW // 2):
            r0 = (half * (KEYS_PER_ROW // 2) + jj) * 8
            cols.append(xt[:, r0:r0 + 8, :].reshape(n_pairs * 8, LANES))
        halves.append(jnp.concatenate(cols, axis=1))
    return jnp.concatenate(halves, axis=0).astype(BF16)


def _attend_one(qat, qrt, knew, rnew, kbuf, rbuf):
    lane_a = lax.broadcasted_iota(jnp.int32, qat.shape, 1) // N_HEADS
    lane_r = lax.broadcasted_iota(jnp.int32, qrt.shape, 1) // N_HEADS
    zero = jnp.zeros((), BF16)
    wk = jnp.concatenate([jnp.where(lane_a == j, qat, zero) for j in range(KEYS_PER_ROW)], axis=0)
    wr = jnp.concatenate([jnp.where(lane_r == j, qrt, zero) for j in range(KEYS_PER_ROW)], axis=0)

    kb = _latent_rows(kbuf)
    rb = _rotary_rows(rbuf)
    hr = kb.shape[0] // 2
    s_lat = [_dot(kb[:hr], wk), _dot(kb[hr:], wk)]
    s_rot = [_dot(rb[:hr], wr), _dot(rb[hr:], wr)]
    s2 = jnp.concatenate([s_lat[0] + s_rot[0], s_lat[1] + s_rot[1]], axis=0)

    knew_b = knew.astype(BF16)
    rnew_b = rnew.astype(BF16)
    s_new = (_dot(jnp.broadcast_to(knew_b, (8, KV_RANK)), qat)
             + _dot(jnp.broadcast_to(rnew_b, (8, QK_ROPE)), qrt))

    m = jnp.max(s2, axis=0, keepdims=True)
    m = _lane_group_reduce(jnp.broadcast_to(m, (8, LANES)), jnp.maximum)
    m = jnp.maximum(m, s_new)
    p2 = jnp.exp2(s2 - m[0:1])
    p_new = jnp.exp2(s_new - m)
    l_lane = _lane_group_reduce(
        jnp.broadcast_to(jnp.sum(p2, axis=0, keepdims=True), (8, LANES)), jnp.add)
    p_new_col = _diag_to_col(p_new)
    l_col = _diag_to_col(l_lane) + p_new_col

    p2t = p2.T.astype(BF16)
    hs = KEYS_PER_ROW // 2
    hp, hk = hs * N_HEADS, hs * KV_RANK
    g_halves = [_dot(p2t[:hp], kb[:, :hk]), _dot(p2t[hp:], kb[:, hk:])]
    o = p_new_col.astype(BF16).astype(F32) * knew_b.astype(F32)
    for g in g_halves:
        for j in range(hs):
            o = o + g[j * N_HEADS:(j + 1) * N_HEADS, j * KV_RANK:(j + 1) * KV_RANK]
    return o / l_col


def _sample_attn_kernel(pt_ref, qat_ref, qrt_ref, knew_ref, rnew_ref, ck_hbm, krt_hbm, o_ref,
                        kbuf_a, rbuf_a, kbuf_b, rbuf_b, sems, *, n_pages):
    g = pl.program_id(0)
    ng = pl.num_programs(0)
    bufs = ((kbuf_a, rbuf_a), (kbuf_b, rbuf_b))

    def start_pair(seq, s, q):
        kbuf, rbuf = bufs[s]
        for par in range(2):
            pid = pt_ref[seq * n_pages + 2 * q + par]
            pltpu.make_async_copy(ck_hbm.at[pid], kbuf.at[q, par], sems.at[0, s]).start()
            pltpu.make_async_copy(krt_hbm.at[pid], rbuf.at[q, par * QK_ROPE:(par + 1) * QK_ROPE, :],
                                  sems.at[1, s]).start()

    def start_inline(seq, s):
        for q in range(n_pages // 2):
            start_pair(seq, s, q)

    def wait_all(s):
        kbuf, rbuf = bufs[s]
        pltpu.make_async_copy(kbuf, kbuf, sems.at[0, s]).wait()
        pltpu.make_async_copy(rbuf, rbuf, sems.at[1, s]).wait()

    @pl.when(g == 0)
    def _():
        def body(q, c):
            start_pair(0, 0, q)
            return c
        lax.fori_loop(0, n_pages // 2, body, 0)

    first = 2 * g
    wait_all(0)
    start_inline(first + 1, 1)
    o_ref[0] = _attend_one(qat_ref[0], qrt_ref[0], knew_ref[0], rnew_ref[0], kbuf_a, rbuf_a)
    wait_all(1)
    start_inline(jnp.minimum(first + 2, 2 * ng - 2), 0)
    o_ref[1] = _attend_one(qat_ref[1], qrt_ref[1], knew_ref[1], rnew_ref[1], kbuf_b, rbuf_b)

    @pl.when(g == ng - 1)
    def _():
        wait_all(0)


def _const_spec(shape):
    zeros = (0,) * len(shape)
    return pl.BlockSpec(shape, lambda *_: zeros)


def _prep_weights(g_pre, w_in, g_q, w_uq, g_kv, w_uk, w_uv, w_o_att, w_pool, pool_scale,
                  w_o_pool, w_out, g_final):
    d_model = w_in.shape[0]
    att_w = N_HEADS * V_DIM
    half = QK_ROPE // 2
    o_kv = Q_RANK
    o_kr = o_kv + KV_RANK
    o_za = o_kr + QK_ROPE
    o_up = o_za + att_w
    o_zp = o_up + POOL_WIDTH
    o_ga = o_zp + POOL_WIDTH
    o_gp = o_ga + d_model
    w_t = w_in.T
    w_kr = w_t[o_kr:o_za]
    w_a = jnp.concatenate([w_t[:o_kr], w_kr, -w_kr[half:], w_kr[:half]], axis=0)
    uq = w_uq.reshape(Q_RANK, N_HEADS, QK_NOPE + QK_ROPE)
    w_qn = uq[:, :, :QK_NOPE].reshape(Q_RANK, N_HEADS * QK_NOPE)
    x1, x2 = uq[:, :, QK_NOPE:QK_NOPE + half], uq[:, :, QK_NOPE + half:]
    w_qr = jnp.concatenate([x1, x2, -x2, x1], axis=2).reshape(Q_RANK, N_HEADS * LANES)
    bf = lambda a: a.astype(BF16)
    row = lambda a: a.reshape(1, -1).astype(F32)
    return dict(
        g_pre=row(g_pre), w_a=bf(w_a), w_za=bf(w_t[o_za:o_up]), w_up=bf(w_t[o_up:o_zp]),
        w_zp=bf(w_t[o_zp:o_ga]), w_ga=bf(w_t[o_ga:o_gp]), w_gp=bf(w_t[o_gp:]),
        g_q=row(g_q), w_qn=bf(w_qn), w_qr=bf(w_qr),
        w_ukt=bf(jnp.transpose(w_uk, (1, 2, 0))),
        g_kv=row(g_kv), w_pool=bf(w_pool), pscale=row(pool_scale), w_opool=bf(w_o_pool),
        w_uv=bf(jnp.transpose(w_uv, (1, 0, 2))),
        w_oatt=bf(w_o_att), w_out=bf(w_out), g_final=row(g_final))


_PROJ_WEIGHTS = ("g_pre", "w_a", "w_za", "w_up", "w_zp", "w_ga", "w_gp", "g_q", "w_qn", "w_qr",
                 "w_ukt", "g_kv", "w_pool", "pscale", "w_opool")


def _cos_sin(pos):
    half = QK_ROPE // 2
    inv = ROPE_BASE ** (-jnp.arange(half, dtype=F32) / half)
    ang = pos.astype(F32)[:, None] * inv[None, :]
    return jnp.cos(ang), jnp.sin(ang)


def _rope_table(pos):
    c, s = _cos_sin(pos)
    return jnp.concatenate([c, c, s, s], axis=1)


ROPE_BLOCK = 128


def _rope_factors(seq):
    ca, sa = _cos_sin(jnp.arange(0, seq, ROPE_BLOCK, dtype=jnp.int32))
    cb, sb = _cos_sin(jnp.arange(ROPE_BLOCK, dtype=jnp.int32))
    rope_a = jnp.stack([jnp.concatenate([ca, ca, sa, sa], axis=1),
                        jnp.concatenate([-sa, -sa, ca, ca], axis=1)], axis=1)
    rope_b = jnp.stack([jnp.concatenate([cb] * 4, axis=1),
                        jnp.concatenate([sb] * 4, axis=1)], axis=0)
    return rope_a, rope_b


def _prompt_proj(x, rope_a, rope_b, hist, wts, tm, t_attn):
    seq, d_model = x.shape
    n = seq // tm
    per_attn = t_attn // tm
    blocks = tm // ROPE_BLOCK
    hist_rows = hist.shape[0]
    ws = [wts[k] for k in _PROJ_WEIGHTS]
    rows = lambda width: pl.BlockSpec((tm, width), lambda i: (i, 0))
    out_shape = (
        jax.ShapeDtypeStruct((N_HEADS, seq, QCAT), BF16),
        jax.ShapeDtypeStruct((seq, QCAT), BF16),
        jax.ShapeDtypeStruct((seq // t_attn, KV_RANK, t_attn), BF16),
        jax.ShapeDtypeStruct((seq, KV_RANK), F32),
        jax.ShapeDtypeStruct((QK_ROPE, seq), F32),
        jax.ShapeDtypeStruct((seq, d_model), BF16),
        jax.ShapeDtypeStruct((seq, d_model), BF16),
        jax.ShapeDtypeStruct((seq, d_model), BF16),
        jax.ShapeDtypeStruct((hist_rows, POOL_WIDTH), F32),
        jax.ShapeDtypeStruct((n, 8, LANES), F32),
    )
    out_specs = (
        pl.BlockSpec((N_HEADS, tm, QCAT), lambda i: (0, i, 0)),
        rows(QCAT),
        pl.BlockSpec((1, KV_RANK, tm), lambda i: (i // per_attn, 0, i % per_attn)),
        rows(KV_RANK), pl.BlockSpec((QK_ROPE, tm), lambda i: (0, i)),
        rows(d_model), rows(d_model), rows(d_model),
        _const_spec((hist_rows, POOL_WIDTH)),
        pl.BlockSpec((1, 8, LANES), lambda i: (i, 0, 0)),
    )
    return pl.pallas_call(
        functools.partial(_prompt_proj_kernel, tm=tm),
        grid=(n,),
        in_specs=[rows(d_model), pl.BlockSpec((blocks, 2, LANES), lambda i: (i, 0, 0)),
                  _const_spec(rope_b.shape), _const_spec(hist.shape)]
        + [_const_spec(w.shape) for w in ws],
        out_specs=out_specs,
        out_shape=out_shape,
        scratch_shapes=[pltpu.VMEM((hist_rows + tm, POOL_WIDTH), F32),
                        pltpu.VMEM((8, LANES), F32)],
        compiler_params=pltpu.CompilerParams(
            dimension_semantics=("arbitrary",), vmem_limit_bytes=VMEM_LIMIT),
    )(x, rope_a, rope_b, hist, *ws)


def _prompt_attn(safe, qcat, kcat, vt, t):
    seq = kcat.shape[0]
    n = seq // t
    grid_spec = pltpu.PrefetchScalarGridSpec(
        num_scalar_prefetch=1,
        grid=(n,),
        in_specs=[pl.BlockSpec((N_HEADS, t, QCAT), lambda i, s: (0, i, 0)),
                  pl.BlockSpec(memory_space=pltpu.VMEM),
                  pl.BlockSpec(memory_space=pltpu.VMEM)],
        out_specs=pl.BlockSpec((t, N_HEADS * KV_RANK), lambda i, s: (i, 0)),
        scratch_shapes=[pltpu.VMEM((N_HEADS, 1, t), F32),
                        pltpu.VMEM((N_HEADS, 1, t), F32),
                        pltpu.VMEM((N_HEADS, KV_RANK, t), F32)],
    )
    return pl.pallas_call(
        functools.partial(_prompt_attn_kernel, t=t),
        grid_spec=grid_spec,
        out_shape=jax.ShapeDtypeStruct((seq, N_HEADS * KV_RANK), BF16),
        compiler_params=pltpu.CompilerParams(
            dimension_semantics=("arbitrary",), vmem_limit_bytes=VMEM_LIMIT),
    )(safe, qcat, kcat, vt)


def _epilogue(o, sz, ga, yp, x, wts, tm):
    n_rows, d_model = x.shape
    rows = pl.BlockSpec((tm, d_model), lambda i: (i, 0))
    ws = [wts[k] for k in ("w_uv", "w_oatt", "w_out", "g_final")]
    return pl.pallas_call(
        _epilogue_kernel,
        grid=(n_rows // tm,),
        in_specs=[rows] * 5 + [_const_spec(w.shape) for w in ws],
        out_specs=rows,
        out_shape=jax.ShapeDtypeStruct((n_rows, d_model), F32),
        compiler_params=pltpu.CompilerParams(
            dimension_semantics=("arbitrary",), vmem_limit_bytes=VMEM_LIMIT),
    )(o, sz, ga, yp, x, *ws)


def _sample_proj(x, cs, hist, wts, past_len):
    b, d_model = x.shape
    ws = [wts[k] for k in _PROJ_WEIGHTS]
    wide = jax.ShapeDtypeStruct((b, d_model), BF16)
    out_shape = (
        jax.ShapeDtypeStruct((b, N_HEADS * LANES), F32),
        jax.ShapeDtypeStruct((b, N_HEADS * LANES), F32),
        jax.ShapeDtypeStruct((b, KV_RANK), F32),
        jax.ShapeDtypeStruct((b, QK_ROPE), F32),
        wide, wide, wide,
        jax.ShapeDtypeStruct(hist.shape, F32),
    )
    args = (x, cs, hist, *ws)
    return pl.pallas_call(
        functools.partial(_sample_proj_kernel, past_len=past_len),
        grid=(1,),
        in_specs=[_const_spec(a.shape) for a in args],
        out_specs=tuple(_const_spec(s.shape) for s in out_shape),
        out_shape=out_shape,
        compiler_params=pltpu.CompilerParams(
            dimension_semantics=("arbitrary",), vmem_limit_bytes=VMEM_LIMIT),
    )(*args)


def _sample_attn(page_table, qat, qrt, knew, rnew, cache_ckv, cache_krope):
    b, n_pages = page_table.shape
    page = cache_ckv.shape[1]
    assert b % 2 == 0 and n_pages % 2 == 0
    pair = lambda rows, width: pl.BlockSpec((2, rows, width), lambda i, pt: (i, 0, 0))
    kbuf = pltpu.VMEM((n_pages // 2, 2, page, KV_RANK), F32)
    rbuf = pltpu.VMEM((n_pages // 2, 2 * QK_ROPE, page), F32)
    grid_spec = pltpu.PrefetchScalarGridSpec(
        num_scalar_prefetch=1,
        grid=(b // 2,),
        in_specs=[pair(KV_RANK, LANES), pair(QK_ROPE, LANES), pair(1, KV_RANK), pair(1, QK_ROPE),
                  pl.BlockSpec(memory_space=pl.ANY),
                  pl.BlockSpec(memory_space=pl.ANY)],
        out_specs=pair(N_HEADS, KV_RANK),
        scratch_shapes=[kbuf, rbuf, kbuf, rbuf, pltpu.SemaphoreType.DMA((2, 2))],
    )
    return pl.pallas_call(
        functools.partial(_sample_attn_kernel, n_pages=n_pages),
        grid_spec=grid_spec,
        out_shape=jax.ShapeDtypeStruct((b, N_HEADS, KV_RANK), F32),
        compiler_params=pltpu.CompilerParams(
            dimension_semantics=("arbitrary",), vmem_limit_bytes=VMEM_LIMIT),
    )(page_table.reshape(-1), qat, qrt, knew, rnew, cache_ckv, jnp.swapaxes(cache_krope, 1, 2))


def kernel(x_prompt, x_sample, cache_ckv, cache_krope, state_pool, page_table, g_pre, w_in, g_q,
           w_uq, g_kv, w_uk, w_uv, w_o_att, w_pool, pool_scale, w_o_pool, w_out, g_final):
    depth = g_pre.shape[0]
    batch, seq, d_model = x_prompt.shape
    dec_b, dec_seq, _ = x_sample.shape
    assert depth == 1 and batch == 1 and dec_seq == 1
    past_len = page_table.shape[1] * cache_ckv.shape[2]
    wts = _prep_weights(g_pre[0], w_in[0], g_q[0], w_uq[0], g_kv[0], w_uk[0], w_uv[0], w_o_att[0],
                        w_pool[0], pool_scale[0], w_o_pool[0], w_out[0], g_final)

    tile = 512
    proj_tile = 512
    hist_rows = 16
    xp = x_prompt.reshape(seq, d_model)
    rope_a, rope_b = _rope_factors(seq)
    hist0 = jnp.zeros((hist_rows, POOL_WIDTH), F32)
    qcat, kcat, vt, ckv_p, krt_p, sz, ga, yp, pstate, slack = _prompt_proj(
        xp, rope_a, rope_b, hist0, wts, proj_tile, tile)
    slack = jnp.max(slack[:, 0, 0].reshape(seq // tile, tile // proj_tile), axis=1)
    safe = (slack <= SAFE_SLACK).astype(jnp.int32)
    o_p = _prompt_attn(safe, qcat, kcat, vt, tile)
    y_p = _epilogue(o_p, sz, ga, yp, xp, wts, tile)

    xs = x_sample.reshape(dec_b, d_model)
    cs_s = _rope_table(jnp.full((dec_b,), past_len, jnp.int32))
    hist_s = jnp.transpose(state_pool[0], (1, 0, 2))
    qa, qr, ckv_s, kr_s, sz_s, ga_s, yp_s, pstate_s = _sample_proj(xs, cs_s, hist_s, wts, past_len)
    qa3 = qa.reshape(dec_b, N_HEADS, LANES)
    qr3 = qr.reshape(dec_b, N_HEADS, LANES)[:, :, :QK_ROPE]
    qat = jnp.tile(jnp.transpose(qa3, (0, 2, 1)), (1, 1, KEYS_PER_ROW)).astype(BF16)
    qrt = jnp.tile(jnp.transpose(qr3, (0, 2, 1)), (1, 1, KEYS_PER_ROW)).astype(BF16)
    o_s = _sample_attn(page_table, qat, qrt, ckv_s.reshape(dec_b, 1, KV_RANK),
                       kr_s.reshape(dec_b, 1, QK_ROPE), cache_ckv[0], cache_krope[0])
    o_s = o_s.reshape(dec_b, N_HEADS * KV_RANK).astype(BF16)
    y_s = _epilogue(o_s, sz_s, ga_s, yp_s, xs, wts, dec_b)

    return (y_p.reshape(batch, seq, d_model),
            y_s.reshape(dec_b, 1, d_model),
            ckv_p.reshape(1, batch, seq, KV_RANK),
            krt_p.T.reshape(1, batch, seq, QK_ROPE),
            pstate[hist_rows - POOL_HIST:].reshape(1, batch, POOL_HIST, POOL_WIDTH),
            ckv_s.reshape(1, dec_b, 1, KV_RANK),
            kr_s.reshape(1, dec_b, 1, QK_ROPE),
            jnp.transpose(pstate_s, (1, 0, 2)).reshape(1, dec_b, POOL_HIST, POOL_WIDTH))
```

```python
import functools

import jax
import jax.numpy as jnp
from jax import lax
from jax.experimental import pallas as pl
from jax.experimental.pallas import tpu as pltpu

F32 = jnp.float32
BF16 = jnp.bfloat16

N_HEADS = 8
QK_NOPE = 128
QK_ROPE = 64
V_DIM = 128
Q_RANK = 256
KV_RANK = 128
ROPE_BASE = 10000.0
POOL_WINDOWS = (2, 4, 8, 16)
POOL_GROUP_DIM = 128
POOL_WIDTH = len(POOL_WINDOWS) * POOL_GROUP_DIM
POOL_HIST = 15
EPS = 1e-6
NEG_INF = -1e30
SOFTMAX_SCALE = (QK_NOPE + QK_ROPE) ** -0.5
LOG2_E = 1.4426950408889634
Q_SCALE = SOFTMAX_SCALE * LOG2_E
SAFE_SLACK = 64.0

LANES = 128
KEYS_PER_ROW = LANES // N_HEADS
QCAT = 2 * LANES
VMEM_LIMIT = 56 * 1024 * 1024


def _dot(a, b):
    return jnp.dot(a, b, preferred_element_type=F32)


def _dot_nt(a, b):
    return lax.dot_general(a, b, (((1,), (1,)), ((), ())), preferred_element_type=F32)


def _rms(x, g):
    return x * lax.rsqrt(jnp.mean(x * x, axis=-1, keepdims=True) + EPS) * g


def _silu(x):
    return x * jax.nn.sigmoid(x)


def _rope_pair(t):
    return t + pltpu.roll(t, QK_ROPE, axis=1)


def _project(x, cs, g_pre, w_a, g_q, w_qn, w_qr, w_ukt, g_kv):
    h = _rms(x, g_pre).astype(BF16)
    za = _dot_nt(h, w_a)
    qn = _rms(za[:, :Q_RANK], g_q).astype(BF16)
    ckv = _rms(za[:, Q_RANK:Q_RANK + KV_RANK], g_kv)
    kr2 = _rope_pair(za[:, Q_RANK + KV_RANK:] * cs)
    q_nope = _dot(qn, w_qn)
    q_rope = _dot(qn, w_qr)
    qa, qr = [], []
    for hd in range(N_HEADS):
        sl = slice(hd * LANES, (hd + 1) * LANES)
        qa.append(_dot(q_nope[:, sl].astype(BF16), w_ukt[hd]) * Q_SCALE)
        qr.append(_rope_pair(q_rope[:, sl] * cs) * Q_SCALE)
    return h, qa, qr, ckv, kr2


def _pool_branch(h, win_sum, u, rcnt, w_zp, w_gp, w_pool, pool_scale, w_o_pool):
    mixed = []
    for g in range(len(POOL_WINDOWS)):
        sl = slice(g * POOL_GROUP_DIM, (g + 1) * POOL_GROUP_DIM)
        pooled = win_sum[g] * rcnt[g] - u[:, sl]
        mixed.append(_dot(pooled.astype(BF16), w_pool[g]))
    p = jnp.concatenate(mixed, axis=1) * pool_scale
    z_pool = _dot_nt(h, w_zp)
    y_pool = _dot((p * _silu(z_pool)).astype(BF16), w_o_pool)
    return jax.nn.sigmoid(_dot_nt(h, w_gp)) * y_pool


def _prompt_proj_kernel(x_ref, rope_a_ref, rope_b_ref, hist_ref, g_pre_ref, w_a_ref, w_za_ref, w_up_ref,
                        w_zp_ref, w_ga_ref, w_gp_ref, g_q_ref, w_qn_ref, w_qr_ref, w_ukt_ref,
                        g_kv_ref, w_pool_ref, pscale_ref, w_opool_ref,
                        qcat_ref, kcat_ref, vt_ref, ckv_ref, kr_ref, sz_ref, ga_ref, yp_ref,
                        pstate_ref, slack_ref, uext_ref, kmax_ref, *, tm):
    i = pl.program_id(0)
    hist_rows = uext_ref.shape[0] - tm

    @pl.when(i == 0)
    def _():
        uext_ref[0:hist_rows, :] = hist_ref[...]
        kmax_ref[...] = jnp.zeros(kmax_ref.shape, F32)

    cs = jnp.concatenate(
        [rope_a_ref[a, 0:1, :] * rope_b_ref[0] + rope_a_ref[a, 1:2, :] * rope_b_ref[1]
         for a in range(rope_a_ref.shape[0])], axis=0)
    h, qa, qr, ckv, kr2 = _project(x_ref[...], cs, g_pre_ref[...], w_a_ref[...], g_q_ref[...],
                                   w_qn_ref[...], w_qr_ref[...], w_ukt_ref, g_kv_ref[...])
    lane = lax.broadcasted_iota(jnp.int32, kr2.shape, 1)
    is_rope = lane < QK_ROPE
    is_shift = lane == QK_ROPE
    rsum = lambda a: jnp.sum(a, axis=1, keepdims=True)
    ckv_b = ckv.astype(BF16).astype(F32)
    kr_b = jnp.where(is_rope, kr2, 0.0).astype(BF16).astype(F32)
    k_norm2 = jnp.max(rsum(ckv_b * ckv_b + kr_b * kr_b), axis=0, keepdims=True)
    k_norm2 = jnp.maximum(k_norm2, kmax_ref[0:1, 0:1])
    kmax_ref[...] = jnp.broadcast_to(k_norm2, kmax_ref.shape)
    slack = None
    for hd in range(N_HEADS):
        qa_b = qa[hd].astype(BF16)
        qr_b = jnp.where(is_rope, qr[hd], 0.0).astype(BF16).astype(F32)
        qa_f = qa_b.astype(F32)
        c = jnp.sqrt(rsum(qa_f * qa_f + qr_b * qr_b) * k_norm2) + 1.0
        c = c.astype(BF16).astype(F32)
        own = rsum(qa_f * ckv_b + qr_b * kr_b)
        worst = jnp.max(c - own, axis=0, keepdims=True)
        slack = worst if slack is None else jnp.maximum(slack, worst)
        qcat_ref[hd, :, 0:LANES] = qa_b
        qcat_ref[hd, :, LANES:QCAT] = jnp.where(is_shift, -c, qr_b).astype(BF16)
    slack_ref[0] = jnp.broadcast_to(slack, slack_ref.shape[1:])
    kcat_ref[:, 0:LANES] = ckv.astype(BF16)
    kcat_ref[:, LANES:QCAT] = jnp.where(is_shift, 1.0, kr_b).astype(BF16)
    vt_ref[0] = ckv.T.astype(BF16)
    ckv_ref[...] = ckv
    kr_ref[...] = kr2.T[:QK_ROPE, :]

    sz_ref[...] = _silu(_dot_nt(h, w_za_ref[...])).astype(sz_ref.dtype)
    ga_ref[...] = jax.nn.sigmoid(_dot_nt(h, w_ga_ref[...])).astype(ga_ref.dtype)

    u = _dot_nt(h, w_up_ref[...])
    uext_ref[hist_rows:, :] = u
    pos = i * tm + lax.broadcasted_iota(jnp.int32, (tm, 1), 0)
    win_sum, rcnt = [], []
    for g, w in enumerate(POOL_WINDOWS):
        c0 = g * POOL_GROUP_DIM
        acc = u[:, c0:c0 + POOL_GROUP_DIM]
        for s in range(1, w):
            acc = acc + uext_ref[hist_rows - s:hist_rows - s + tm, c0:c0 + POOL_GROUP_DIM]
        win_sum.append(acc)
        rcnt.append(1.0 / jnp.minimum(w, pos + 1).astype(F32))
    yp_ref[...] = _pool_branch(h, win_sum, u, rcnt, w_zp_ref[...], w_gp_ref[...], w_pool_ref,
                               pscale_ref[...], w_opool_ref[...]).astype(yp_ref.dtype)
    tail = uext_ref[tm:tm + hist_rows, :]
    pstate_ref[...] = tail
    uext_ref[0:hist_rows, :] = tail


def _prompt_attn_kernel(safe_ref, q_ref, k_ref, vt_ref, o_ref, m_sc, l_sc, acc_sc, *, t):
    i = pl.program_id(0)
    l_sc[...] = jnp.zeros(l_sc.shape, F32)
    acc_sc[...] = jnp.zeros(acc_sc.shape, F32)

    def chunks(specs, shifted):
        kcs = [k_ref[pl.ds(pl.multiple_of(j * t, t), t), :] for j, _ in specs]
        vts = [vt_ref[j] for j, _ in specs]
        units = [(c, hd) for c in range(len(specs)) for hd in range(N_HEADS)]
        if any(masked for _, masked in specs):
            key = lax.broadcasted_iota(jnp.int32, (t, t), 0)
            qry = lax.broadcasted_iota(jnp.int32, (t, t), 1)
            keep = key <= qry
        st_next = _dot_nt(kcs[0], q_ref[0])
        for u, (c, hd) in enumerate(units):
            st = st_next
            if u + 1 < len(units):
                cn, hn = units[u + 1]
                st_next = _dot_nt(kcs[cn], q_ref[hn])
            if specs[c][1]:
                st = jnp.where(keep, st, NEG_INF)
            if shifted:
                p = jnp.exp2(st)
                l_sc[hd] += jnp.sum(p, axis=0, keepdims=True)
                acc_sc[hd] += _dot(vts[c], p.astype(BF16))
            else:
                m_prev = m_sc[hd]
                m_new = jnp.maximum(m_prev, jnp.max(st, axis=0, keepdims=True))
                alpha = jnp.exp2(m_prev - m_new)
                p = jnp.exp2(st - m_new)
                l_sc[hd] = alpha * l_sc[hd] + jnp.sum(p, axis=0, keepdims=True)
                acc_sc[hd] = alpha * acc_sc[hd] + _dot(vts[c], p.astype(BF16))
                m_sc[hd] = m_new

    def sweep(shifted, width):
        def body(jw, carry):
            chunks([(jw * width + c, False) for c in range(width)], shifted)
            return carry
        lax.fori_loop(0, i // width, body, 0)
        if width == 1:
            chunks([(i, True)], shifted)
        else:
            assert width == 2
            odd = i % 2 == 1

            @pl.when(odd)
            def _():
                chunks([(i - 1, False), (i, True)], shifted)

            @pl.when(jnp.logical_not(odd))
            def _():
                chunks([(i, True)], shifted)

    safe = safe_ref[i] != 0

    @pl.when(safe)
    def _():
        sweep(True, 2)

    @pl.when(jnp.logical_not(safe))
    def _():
        m_sc[...] = jnp.full(m_sc.shape, NEG_INF, F32)
        sweep(False, 1)

    for hd in range(N_HEADS):
        o = acc_sc[hd] / l_sc[hd]
        o_ref[:, hd * KV_RANK:(hd + 1) * KV_RANK] = o.T.astype(BF16)


def _epilogue_kernel(o_ref, sz_ref, ga_ref, yp_ref, x_ref, w_uv_ref, w_oatt_ref, w_out_ref,
                     g_final_ref, y_ref):
    o = o_ref[...]
    o2 = jnp.concatenate(
        [_dot(o[:, hd * KV_RANK:(hd + 1) * KV_RANK], w_uv_ref[hd]) for hd in range(N_HEADS)], axis=1)
    y_att = _dot((o2 * sz_ref[...].astype(F32)).astype(BF16), w_oatt_ref[...])
    merged = ga_ref[...].astype(F32) * y_att + yp_ref[...].astype(F32)
    xo = x_ref[...] + _dot(merged.astype(BF16), w_out_ref[...])
    y_ref[...] = _rms(xo, g_final_ref[...])


def _sample_proj_kernel(x_ref, cs_ref, hist_ref, g_pre_ref, w_a_ref, w_za_ref, w_up_ref,
                        w_zp_ref, w_ga_ref, w_gp_ref, g_q_ref, w_qn_ref, w_qr_ref, w_ukt_ref,
                        g_kv_ref, w_pool_ref, pscale_ref, w_opool_ref,
                        qa_ref, qr_ref, ckv_ref, kr_ref, sz_ref, ga_ref, yp_ref, pstate_ref,
                        *, past_len):
    cs = cs_ref[...]
    h, qa, qr, ckv, kr2 = _project(x_ref[...], cs, g_pre_ref[...], w_a_ref[...], g_q_ref[...],
                                   w_qn_ref[...], w_qr_ref[...], w_ukt_ref, g_kv_ref[...])
    for hd in range(N_HEADS):
        qa_ref[:, hd * LANES:(hd + 1) * LANES] = qa[hd]
        qr_ref[:, hd * LANES:(hd + 1) * LANES] = qr[hd]
    ckv_ref[...] = ckv
    kr_ref[...] = kr2[:, :QK_ROPE]
    sz_ref[...] = _silu(_dot_nt(h, w_za_ref[...])).astype(sz_ref.dtype)
    ga_ref[...] = jax.nn.sigmoid(_dot_nt(h, w_ga_ref[...])).astype(ga_ref.dtype)

    u = _dot_nt(h, w_up_ref[...])
    win_sum, rcnt = [], []
    for g, w in enumerate(POOL_WINDOWS):
        c0 = g * POOL_GROUP_DIM
        acc = u[:, c0:c0 + POOL_GROUP_DIM]
        for s in range(1, w):
            acc = acc + hist_ref[POOL_HIST - s, :, c0:c0 + POOL_GROUP_DIM]
        win_sum.append(acc)
        rcnt.append(1.0 / float(min(w, past_len + 1)))
    yp_ref[...] = _pool_branch(h, win_sum, u, rcnt, w_zp_ref[...], w_gp_ref[...], w_pool_ref,
                               pscale_ref[...], w_opool_ref[...]).astype(yp_ref.dtype)
    pstate_ref[0:POOL_HIST - 1] = hist_ref[1:POOL_HIST]
    pstate_ref[POOL_HIST - 1] = u


def _lane_group_reduce(v, op):
    sh = N_HEADS
    while sh < LANES:
        v = op(v, pltpu.roll(v, sh, axis=1))
        sh *= 2
    return v


def _diag_to_col(v):
    sub = lax.broadcasted_iota(jnp.int32, v.shape, 0)
    lane = lax.broadcasted_iota(jnp.int32, v.shape, 1)
    return jnp.sum(jnp.where(sub == lane, v, 0.0), axis=1, keepdims=True)


def _latent_rows(kbuf):
    n_pairs, _, page, width = kbuf.shape
    halves = []
    for half in range(2):
        cols = []
        for jj in range(KEYS_PER_ROW // 2):
            r0 = (half * (KEYS_PER_ROW // 2) + jj) * 8
            for par in range(2):
                cols.append(kbuf[:, par, r0:r0 + 8, :].reshape(n_pairs * 8, width))
        halves.append(jnp.concatenate(cols, axis=1))
    return jnp.concatenate(halves, axis=0).astype(BF16)


def _rotary_rows(rbuf):
    n_pairs = rbuf.shape[0]
    xt = jnp.swapaxes(rbuf[...], 1, 2)
    halves = []
    for half in range(2):
        cols = []
        for jj in range(KEYS_PER_ROW // 2):
            r0 = (half * (KEYS_PER_ROW // 2) + jj) * 8
            cols.append(xt[:, r0:r0 + 8, :].reshape(n_pairs * 8, LANES))
        halves.append(jnp.concatenate(cols, axis=1))
    return jnp.concatenate(halves, axis=0).astype(BF16)


def _attend_one(qat, qrt, knew, rnew, kbuf, rbuf):
    lane_a = lax.broadcasted_iota(jnp.int32, qat.shape, 1) // N_HEADS
    lane_r = lax.broadcasted_iota(jnp.int32, qrt.shape, 1) // N_HEADS
    zero = jnp.zeros((), BF16)
    wk = jnp.concatenate([jnp.where(lane_a == j, qat, zero) for j in range(KEYS_PER_ROW)], axis=0)
    wr = jnp.concatenate([jnp.where(lane_r == j, qrt, zero) for j in range(KEYS_PER_ROW)], axis=0)

    kb = _latent_rows(kbuf)
    rb = _rotary_rows(rbuf)
    hr = kb.shape[0] // 2
    s_lat = [_dot(kb[:hr], wk), _dot(kb[hr:], wk)]
    s_rot = [_dot(rb[:hr], wr), _dot(rb[hr:], wr)]
    s2 = jnp.concatenate([s_lat[0] + s_rot[0], s_lat[1] + s_rot[1]], axis=0)

    knew_b = knew.astype(BF16)
    rnew_b = rnew.astype(BF16)
    s_new = (_dot(jnp.broadcast_to(knew_b, (8, KV_RANK)), qat)
             + _dot(jnp.broadcast_to(rnew_b, (8, QK_ROPE)), qrt))

    m = jnp.max(s2, axis=0, keepdims=True)
    m = _lane_group_reduce(jnp.broadcast_to(m, (8, LANES)), jnp.maximum)
    m = jnp.maximum(m, s_new)
    p2 = jnp.exp2(s2 - m[0:1])
    p_new = jnp.exp2(s_new - m)
    l_lane = _lane_group_reduce(
        jnp.broadcast_to(jnp.sum(p2, axis=0, keepdims=True), (8, LANES)), jnp.add)
    p_new_col = _diag_to_col(p_new)
    l_col = _diag_to_col(l_lane) + p_new_col

    p2t = p2.T.astype(BF16)
    hs = KEYS_PER_ROW // 2
    hp, hk = hs * N_HEADS, hs * KV_RANK
    g_halves = [_dot(p2t[:hp], kb[:, :hk]), _dot(p2t[hp:], kb[:, hk:])]
    o = p_new_col.astype(BF16).astype(F32) * knew_b.astype(F32)
    for g in g_halves:
        for j in range(hs):
            o = o + g[j * N_HEADS:(j + 1) * N_HEADS, j * KV_RANK:(j + 1) * KV_RANK]
    return o / l_col


def _sample_attn_kernel(pt_ref, qat_ref, qrt_ref, knew_ref, rnew_ref, ck_hbm, krt_hbm, o_ref,
                        kbuf_a, rbuf_a, kbuf_b, rbuf_b, sems, *, n_pages):
    g = pl.program_id(0)
    ng = pl.num_programs(0)
    bufs = ((kbuf_a, rbuf_a), (kbuf_b, rbuf_b))

    def start_pair(seq, s, q):
        kbuf, rbuf = bufs[s]
        for par in range(2):
            pid = pt_ref[seq * n_pages + 2 * q + par]
            pltpu.make_async_copy(ck_hbm.at[pid], kbuf.at[q, par], sems.at[0, s]).start()
            pltpu.make_async_copy(krt_hbm.at[pid], rbuf.at[q, par * QK_ROPE:(par + 1) * QK_ROPE, :],
                                  sems.at[1, s]).start()

    def start_inline(seq, s):
        for q in range(n_pages // 2):
            start_pair(seq, s, q)

    def wait_all(s):
        kbuf, rbuf = bufs[s]
        pltpu.make_async_copy(kbuf, kbuf, sems.at[0, s]).wait()
        pltpu.make_async_copy(rbuf, rbuf, sems.at[1, s]).wait()

    @pl.when(g == 0)
    def _():
        def body(q, c):
            start_pair(0, 0, q)
            start_pair(1, 1, q)
            return c
        lax.fori_loop(0, n_pages // 2, body, 0)

    nxt = jnp.minimum(g + 1, ng - 1)
    wait_all(0)
    o_ref[0] = _attend_one(qat_ref[0], qrt_ref[0], knew_ref[0], rnew_ref[0], kbuf_a, rbuf_a)
    start_inline(2 * nxt, 0)
    wait_all(1)
    o_ref[1] = _attend_one(qat_ref[1], qrt_ref[1], knew_ref[1], rnew_ref[1], kbuf_b, rbuf_b)
    start_inline(2 * nxt + 1, 1)

    @pl.when(g == ng - 1)
    def _():
        wait_all(0)
        wait_all(1)


def _const_spec(shape):
    zeros = (0,) * len(shape)
    return pl.BlockSpec(shape, lambda *_: zeros)


def _prep_weights(g_pre, w_in, g_q, w_uq, g_kv, w_uk, w_uv, w_o_att, w_pool, pool_scale,
                  w_o_pool, w_out, g_final):
    d_model = w_in.shape[0]
    att_w = N_HEADS * V_DIM
    half = QK_ROPE // 2
    o_kv = Q_RANK
    o_kr = o_kv + KV_RANK
    o_za = o_kr + QK_ROPE
    o_up = o_za + att_w
    o_zp = o_up + POOL_WIDTH
    o_ga = o_zp + POOL_WIDTH
    o_gp = o_ga + d_model
    w_t = w_in.T
    w_kr = w_t[o_kr:o_za]
    w_a = jnp.concatenate([w_t[:o_kr], w_kr, -w_kr[half:], w_kr[:half]], axis=0)
    uq = w_uq.reshape(Q_RANK, N_HEADS, QK_NOPE + QK_ROPE)
    w_qn = uq[:, :, :QK_NOPE].reshape(Q_RANK, N_HEADS * QK_NOPE)
    x1, x2 = uq[:, :, QK_NOPE:QK_NOPE + half], uq[:, :, QK_NOPE + half:]
    w_qr = jnp.concatenate([x1, x2, -x2, x1], axis=2).reshape(Q_RANK, N_HEADS * LANES)
    bf = lambda a: a.astype(BF16)
    row = lambda a: a.reshape(1, -1).astype(F32)
    return dict(
        g_pre=row(g_pre), w_a=bf(w_a), w_za=bf(w_t[o_za:o_up]), w_up=bf(w_t[o_up:o_zp]),
        w_zp=bf(w_t[o_zp:o_ga]), w_ga=bf(w_t[o_ga:o_gp]), w_gp=bf(w_t[o_gp:]),
        g_q=row(g_q), w_qn=bf(w_qn), w_qr=bf(w_qr),
        w_ukt=bf(jnp.transpose(w_uk, (1, 2, 0))),
        g_kv=row(g_kv), w_pool=bf(w_pool), pscale=row(pool_scale), w_opool=bf(w_o_pool),
        w_uv=bf(jnp.transpose(w_uv, (1, 0, 2))),
        w_oatt=bf(w_o_att), w_out=bf(w_out), g_final=row(g_final))


_PROJ_WEIGHTS = ("g_pre", "w_a", "w_za", "w_up", "w_zp", "w_ga", "w_gp", "g_q", "w_qn", "w_qr",
                 "w_ukt", "g_kv", "w_pool", "pscale", "w_opool")


def _cos_sin(pos):
    half = QK_ROPE // 2
    inv = ROPE_BASE ** (-jnp.arange(half, dtype=F32) / half)
    ang = pos.astype(F32)[:, None] * inv[None, :]
    return jnp.cos(ang), jnp.sin(ang)


def _rope_table(pos):
    c, s = _cos_sin(pos)
    return jnp.concatenate([c, c, s, s], axis=1)


ROPE_BLOCK = 128


def _rope_factors(seq):
    ca, sa = _cos_sin(jnp.arange(0, seq, ROPE_BLOCK, dtype=jnp.int32))
    cb, sb = _cos_sin(jnp.arange(ROPE_BLOCK, dtype=jnp.int32))
    rope_a = jnp.stack([jnp.concatenate([ca, ca, sa, sa], axis=1),
                        jnp.concatenate([-sa, -sa, ca, ca], axis=1)], axis=1)
    rope_b = jnp.stack([jnp.concatenate([cb] * 4, axis=1),
                        jnp.concatenate([sb] * 4, axis=1)], axis=0)
    return rope_a, rope_b


def _prompt_proj(x, rope_a, rope_b, hist, wts, tm, t_attn):
    seq, d_model = x.shape
    n = seq // tm
    per_attn = t_attn // tm
    blocks = tm // ROPE_BLOCK
    hist_rows = hist.shape[0]
    ws = [wts[k] for k in _PROJ_WEIGHTS]
    rows = lambda width: pl.BlockSpec((tm, width), lambda i: (i, 0))
    out_shape = (
        jax.ShapeDtypeStruct((N_HEADS, seq, QCAT), BF16),
        jax.ShapeDtypeStruct((seq, QCAT), BF16),
        jax.ShapeDtypeStruct((seq // t_attn, KV_RANK, t_attn), BF16),
        jax.ShapeDtypeStruct((seq, KV_RANK), F32),
        jax.ShapeDtypeStruct((QK_ROPE, seq), F32),
        jax.ShapeDtypeStruct((seq, d_model), BF16),
        jax.ShapeDtypeStruct((seq, d_model), BF16),
        jax.ShapeDtypeStruct((seq, d_model), BF16),
        jax.ShapeDtypeStruct((hist_rows, POOL_WIDTH), F32),
        jax.ShapeDtypeStruct((n, 8, LANES), F32),
    )
    out_specs = (
        pl.BlockSpec((N_HEADS, tm, QCAT), lambda i: (0, i, 0)),
        rows(QCAT),
        pl.BlockSpec((1, KV_RANK, tm), lambda i: (i // per_attn, 0, i % per_attn)),
        rows(KV_RANK), pl.BlockSpec((QK_ROPE, tm), lambda i: (0, i)),
        rows(d_model), rows(d_model), rows(d_model),
        _const_spec((hist_rows, POOL_WIDTH)),
        pl.BlockSpec((1, 8, LANES), lambda i: (i, 0, 0)),
    )
    return pl.pallas_call(
        functools.partial(_prompt_proj_kernel, tm=tm),
        grid=(n,),
        in_specs=[rows(d_model), pl.BlockSpec((blocks, 2, LANES), lambda i: (i, 0, 0)),
                  _const_spec(rope_b.shape), _const_spec(hist.shape)]
        + [_const_spec(w.shape) for w in ws],
        out_specs=out_specs,
        out_shape=out_shape,
        scratch_shapes=[pltpu.VMEM((hist_rows + tm, POOL_WIDTH), F32),
                        pltpu.VMEM((8, LANES), F32)],
        compiler_params=pltpu.CompilerParams(
            dimension_semantics=("arbitrary",), vmem_limit_bytes=VMEM_LIMIT),
    )(x, rope_a, rope_b, hist, *ws)


def _prompt_attn(safe, qcat, kcat, vt, t):
    seq = kcat.shape[0]
    n = seq // t
    grid_spec = pltpu.PrefetchScalarGridSpec(
        num_scalar_prefetch=1,
        grid=(n,),
        in_specs=[pl.BlockSpec((N_HEADS, t, QCAT), lambda i, s: (0, i, 0)),
                  pl.BlockSpec(memory_space=pltpu.VMEM),
                  pl.BlockSpec(memory_space=pltpu.VMEM)],
        out_specs=pl.BlockSpec((t, N_HEADS * KV_RANK), lambda i, s: (i, 0)),
        scratch_shapes=[pltpu.VMEM((N_HEADS, 1, t), F32),
                        pltpu.VMEM((N_HEADS, 1, t), F32),
                        pltpu.VMEM((N_HEADS, KV_RANK, t), F32)],
    )
    return pl.pallas_call(
        functools.partial(_prompt_attn_kernel, t=t),
        grid_spec=grid_spec,
        out_shape=jax.ShapeDtypeStruct((seq, N_HEADS * KV_RANK), BF16),
        compiler_params=pltpu.CompilerParams(
            dimension_semantics=("arbitrary",), vmem_limit_bytes=VMEM_LIMIT),
    )(safe, qcat, kcat, vt)


def _epilogue(o, sz, ga, yp, x, wts, tm):
    n_rows, d_model = x.shape
    rows = pl.BlockSpec((tm, d_model), lambda i: (i, 0))
    ws = [wts[k] for k in ("w_uv", "w_oatt", "w_out", "g_final")]
    return pl.pallas_call(
        _epilogue_kernel,
        grid=(n_rows // tm,),
        in_specs=[rows] * 5 + [_const_spec(w.shape) for w in ws],
        out_specs=rows,
        out_shape=jax.ShapeDtypeStruct((n_rows, d_model), F32),
        compiler_params=pltpu.CompilerParams(
            dimension_semantics=("arbitrary",), vmem_limit_bytes=VMEM_LIMIT),
    )(o, sz, ga, yp, x, *ws)


def _sample_proj(x, cs, hist, wts, past_len):
    b, d_model = x.shape
    ws = [wts[k] for k in _PROJ_WEIGHTS]
    wide = jax.ShapeDtypeStruct((b, d_model), BF16)
    out_shape = (
        jax.ShapeDtypeStruct((b, N_HEADS * LANES), F32),
        jax.ShapeDtypeStruct((b, N_HEADS * LANES), F32),
        jax.ShapeDtypeStruct((b, KV_RANK), F32),
        jax.ShapeDtypeStruct((b, QK_ROPE), F32),
        wide, wide, wide,
        jax.ShapeDtypeStruct(hist.shape, F32),
    )
    args = (x, cs, hist, *ws)
    return pl.pallas_call(
        functools.partial(_sample_proj_kernel, past_len=past_len),
        grid=(1,),
        in_specs=[_const_spec(a.shape) for a in args],
        out_specs=tuple(_const_spec(s.shape) for s in out_shape),
        out_shape=out_shape,
        compiler_params=pltpu.CompilerParams(
            dimension_semantics=("arbitrary",), vmem_limit_bytes=VMEM_LIMIT),
    )(*args)


def _sample_attn(page_table, qat, qrt, knew, rnew, cache_ckv, cache_krope):
    b, n_pages = page_table.shape
    page = cache_ckv.shape[1]
    assert b % 2 == 0 and n_pages % 2 == 0
    pair = lambda rows, width: pl.BlockSpec((2, rows, width), lambda i, pt: (i, 0, 0))
    kbuf = pltpu.VMEM((n_pages // 2, 2, page, KV_RANK), F32)
    rbuf = pltpu.VMEM((n_pages // 2, 2 * QK_ROPE, page), F32)
    grid_spec = pltpu.PrefetchScalarGridSpec(
        num_scalar_prefetch=1,
        grid=(b // 2,),
        in_specs=[pair(KV_RANK, LANES), pair(QK_ROPE, LANES), pair(1, KV_RANK), pair(1, QK_ROPE),
                  pl.BlockSpec(memory_space=pl.ANY),
                  pl.BlockSpec(memory_space=pl.ANY)],
        out_specs=pair(N_HEADS, KV_RANK),
        scratch_shapes=[kbuf, rbuf, kbuf, rbuf, pltpu.SemaphoreType.DMA((2, 2))],
    )
    return pl.pallas_call(
        functools.partial(_sample_attn_kernel, n_pages=n_pages),
        grid_spec=grid_spec,
        out_shape=jax.ShapeDtypeStruct((b, N_HEADS, KV_RANK), F32),
        compiler_params=pltpu.CompilerParams(
            dimension_semantics=("arbitrary",), vmem_limit_bytes=VMEM_LIMIT),
    )(page_table.reshape(-1), qat, qrt, knew, rnew, cache_ckv, jnp.swapaxes(cache_krope, 1, 2))


def kernel(x_prompt, x_sample, cache_ckv, cache_krope, state_pool, page_table, g_pre, w_in, g_q,
           w_uq, g_kv, w_uk, w_uv, w_o_att, w_pool, pool_scale, w_o_pool, w_out, g_final):
    depth = g_pre.shape[0]
    batch, seq, d_model = x_prompt.shape
    dec_b, dec_seq, _ = x_sample.shape
    assert depth == 1 and batch == 1 and dec_seq == 1
    past_len = page_table.shape[1] * cache_ckv.shape[2]
    wts = _prep_weights(g_pre[0], w_in[0], g_q[0], w_uq[0], g_kv[0], w_uk[0], w_uv[0], w_o_att[0],
                        w_pool[0], pool_scale[0], w_o_pool[0], w_out[0], g_final)

    tile = 512
    proj_tile = 512
    hist_rows = 16
    xp = x_prompt.reshape(seq, d_model)
    rope_a, rope_b = _rope_factors(seq)
    hist0 = jnp.zeros((hist_rows, POOL_WIDTH), F32)
    qcat, kcat, vt, ckv_p, krt_p, sz, ga, yp, pstate, slack = _prompt_proj(
        xp, rope_a, rope_b, hist0, wts, proj_tile, tile)
    slack = jnp.max(slack[:, 0, 0].reshape(seq // tile, tile // proj_tile), axis=1)
    safe = (slack <= SAFE_SLACK).astype(jnp.int32)
    o_p = _prompt_attn(safe, qcat, kcat, vt, tile)
    y_p = _epilogue(o_p, sz, ga, yp, xp, wts, tile)

    xs = x_sample.reshape(dec_b, d_model)
    cs_s = _rope_table(jnp.full((dec_b,), past_len, jnp.int32))
    hist_s = jnp.transpose(state_pool[0], (1, 0, 2))
    qa, qr, ckv_s, kr_s, sz_s, ga_s, yp_s, pstate_s = _sample_proj(xs, cs_s, hist_s, wts, past_len)
    qa3 = qa.reshape(dec_b, N_HEADS, LANES)
    qr3 = qr.reshape(dec_b, N_HEADS, LANES)[:, :, :QK_ROPE]
    qat = jnp.tile(jnp.transpose(qa3, (0, 2, 1)), (1, 1, KEYS_PER_ROW)).astype(BF16)
    qrt = jnp.tile(jnp.transpose(qr3, (0, 2, 1)), (1, 1, KEYS_PER_ROW)).astype(BF16)
    o_s = _sample_attn(page_table, qat, qrt, ckv_s.reshape(dec_b, 1, KV_RANK),
                       kr_s.reshape(dec_b, 1, QK_ROPE), cache_ckv[0], cache_krope[0])
    o_s = o_s.reshape(dec_b, N_HEADS * KV_RANK).astype(BF16)
    y_s = _epilogue(o_s, sz_s, ga_s, yp_s, xs, wts, dec_b)

    return (y_p.reshape(batch, seq, d_model),
            y_s.reshape(dec_b, 1, d_model),
            ckv_p.reshape(1, batch, seq, KV_RANK),
            krt_p.T.reshape(1, batch, seq, QK_ROPE),
            pstate[hist_rows - POOL_HIST:].reshape(1, batch, POOL_HIST, POOL_WIDTH),
            ckv_s.reshape(1, dec_b, 1, KV_RANK),
            kr_s.reshape(1, dec_b, 1, QK_ROPE),
            jnp.transpose(pstate_s, (1, 0, 2)).reshape(1, dec_b, POOL_HIST, POOL_WIDTH))
```

```python
import functools

import jax
import jax.numpy as jnp
from jax import lax
from jax.experimental import pallas as pl
from jax.experimental.pallas import tpu as pltpu

F32 = jnp.float32
BF16 = jnp.bfloat16

N_HEADS = 8
QK_NOPE = 128
QK_ROPE = 64
V_DIM = 128
Q_RANK = 256
KV_RANK = 128
ROPE_BASE = 10000.0
POOL_WINDOWS = (2, 4, 8, 16)
POOL_GROUP_DIM = 128
POOL_WIDTH = len(POOL_WINDOWS) * POOL_GROUP_DIM
POOL_HIST = 15
EPS = 1e-6
NEG_INF = -1e30
SOFTMAX_SCALE = (QK_NOPE + QK_ROPE) ** -0.5
LOG2_E = 1.4426950408889634
Q_SCALE = SOFTMAX_SCALE * LOG2_E
SAFE_SLACK = 64.0

LANES = 128
KEYS_PER_ROW = LANES // N_HEADS
QCAT = 2 * LANES
VMEM_LIMIT = 56 * 1024 * 1024


def _dot(a, b):
    return jnp.dot(a, b, preferred_element_type=F32)


def _dot_nt(a, b):
    return lax.dot_general(a, b, (((1,), (1,)), ((), ())), preferred_element_type=F32)


def _rms(x, g):
    return x * lax.rsqrt(jnp.mean(x * x, axis=-1, keepdims=True) + EPS) * g


def _silu(x):
    return x * jax.nn.sigmoid(x)


def _rope_pair(t):
    return t + pltpu.roll(t, QK_ROPE, axis=1)


def _queries_keys(za, cs, g_q, w_qn, w_qr, w_ukt, g_kv):
    qn = _rms(za[:, :Q_RANK], g_q).astype(BF16)
    ckv = _rms(za[:, Q_RANK:Q_RANK + KV_RANK], g_kv)
    kr2 = _rope_pair(za[:, Q_RANK + KV_RANK:] * cs)
    q_nope = _dot(qn, w_qn)
    q_rope = _dot(qn, w_qr)
    qa, qr = [], []
    for hd in range(N_HEADS):
        sl = slice(hd * LANES, (hd + 1) * LANES)
        qa.append(_dot(q_nope[:, sl].astype(BF16), w_ukt[hd]) * Q_SCALE)
        qr.append(_rope_pair(q_rope[:, sl] * cs) * Q_SCALE)
    return qa, qr, ckv, kr2


def _pool_branch(h, win_sum, u, rcnt, w_zp, w_gp, w_pool, pool_scale, w_o_pool):
    mixed = []
    for g in range(len(POOL_WINDOWS)):
        sl = slice(g * POOL_GROUP_DIM, (g + 1) * POOL_GROUP_DIM)
        pooled = win_sum[g] * rcnt[g] - u[:, sl]
        mixed.append(_dot(pooled.astype(BF16), w_pool[g]))
    p = jnp.concatenate(mixed, axis=1) * pool_scale
    z_pool = _dot_nt(h, w_zp)
    y_pool = _dot((p * _silu(z_pool)).astype(BF16), w_o_pool)
    return jax.nn.sigmoid(_dot_nt(h, w_gp)) * y_pool


def _gates_and_pool_input(h, w_up_ref, w_za_ref, w_ga_ref, sz_ref, ga_ref):
    sz_ref[...] = _silu(_dot_nt(h, w_za_ref[...])).astype(sz_ref.dtype)
    ga_ref[...] = jax.nn.sigmoid(_dot_nt(h, w_ga_ref[...])).astype(ga_ref.dtype)
    return _dot_nt(h, w_up_ref[...])


def _prompt_proj_kernel(x_ref, rope_a_ref, rope_b_ref, hist_ref, g_pre_ref, w_a_ref, w_za_ref, w_up_ref,
                        w_zp_ref, w_ga_ref, w_gp_ref, g_q_ref, w_qn_ref, w_qr_ref, w_ukt_ref,
                        g_kv_ref, w_pool_ref, pscale_ref, w_opool_ref,
                        qcat_ref, kcat_ref, vt_ref, ckv_ref, kr_ref, sz_ref, ga_ref, yp_ref,
                        pstate_ref, slack_ref, uext_ref, kmax_ref, *, tm):
    i = pl.program_id(0)
    hist_rows = uext_ref.shape[0] - tm

    @pl.when(i == 0)
    def _():
        uext_ref[0:hist_rows, :] = hist_ref[...]
        kmax_ref[...] = jnp.zeros(kmax_ref.shape, F32)

    cs = jnp.concatenate(
        [rope_a_ref[a, 0:1, :] * rope_b_ref[0] + rope_a_ref[a, 1:2, :] * rope_b_ref[1]
         for a in range(rope_a_ref.shape[0])], axis=0)
    h = _rms(x_ref[...], g_pre_ref[...]).astype(BF16)
    za = _dot_nt(h, w_a_ref[...])
    qa, qr, ckv, kr2 = _queries_keys(za, cs, g_q_ref[...], w_qn_ref[...], w_qr_ref[...], w_ukt_ref,
                                     g_kv_ref[...])
    lane = lax.broadcasted_iota(jnp.int32, kr2.shape, 1)
    is_rope = lane < QK_ROPE
    is_shift = lane == QK_ROPE
    rsum = lambda a: jnp.sum(a, axis=1, keepdims=True)
    ckv_b = ckv.astype(BF16).astype(F32)
    kr_b = jnp.where(is_rope, kr2, 0.0).astype(BF16).astype(F32)
    k_norm2 = jnp.max(rsum(ckv_b * ckv_b + kr_b * kr_b), axis=0, keepdims=True)
    k_norm2 = jnp.maximum(k_norm2, kmax_ref[0:1, 0:1])
    kmax_ref[...] = jnp.broadcast_to(k_norm2, kmax_ref.shape)
    slack = None
    for hd in range(N_HEADS):
        qa_b = qa[hd].astype(BF16)
        qr_b = jnp.where(is_rope, qr[hd], 0.0).astype(BF16).astype(F32)
        qa_f = qa_b.astype(F32)
        c = jnp.sqrt(rsum(qa_f * qa_f + qr_b * qr_b) * k_norm2) + 1.0
        c = c.astype(BF16).astype(F32)
        own = rsum(qa_f * ckv_b + qr_b * kr_b)
        worst = jnp.max(c - own, axis=0, keepdims=True)
        slack = worst if slack is None else jnp.maximum(slack, worst)
        qcat_ref[hd, :, 0:LANES] = qa_b
        qcat_ref[hd, :, LANES:QCAT] = jnp.where(is_shift, -c, qr_b).astype(BF16)
    slack_ref[0] = jnp.broadcast_to(slack, slack_ref.shape[1:])
    kcat_ref[:, 0:LANES] = ckv.astype(BF16)
    kcat_ref[:, LANES:QCAT] = jnp.where(is_shift, 1.0, kr_b).astype(BF16)
    vt_ref[0] = ckv.T.astype(BF16)
    ckv_ref[...] = ckv
    kr_ref[...] = kr2.T[:QK_ROPE, :]

    u = _gates_and_pool_input(h, w_up_ref, w_za_ref, w_ga_ref, sz_ref, ga_ref)
    uext_ref[hist_rows:, :] = u
    pos = i * tm + lax.broadcasted_iota(jnp.int32, (tm, 1), 0)
    win_sum, rcnt = [], []
    for g, w in enumerate(POOL_WINDOWS):
        c0 = g * POOL_GROUP_DIM
        acc = u[:, c0:c0 + POOL_GROUP_DIM]
        for s in range(1, w):
            acc = acc + uext_ref[hist_rows - s:hist_rows - s + tm, c0:c0 + POOL_GROUP_DIM]
        win_sum.append(acc)
        rcnt.append(1.0 / jnp.minimum(w, pos + 1).astype(F32))
    yp_ref[...] = _pool_branch(h, win_sum, u, rcnt, w_zp_ref[...], w_gp_ref[...], w_pool_ref,
                               pscale_ref[...], w_opool_ref[...]).astype(yp_ref.dtype)
    tail = uext_ref[tm:tm + hist_rows, :]
    pstate_ref[...] = tail
    uext_ref[0:hist_rows, :] = tail


def _prompt_attn_kernel(safe_ref, q_ref, k_ref, vt_ref, o_ref, m_sc, l_sc, acc_sc, *, t):
    i = pl.program_id(0)
    l_sc[...] = jnp.zeros(l_sc.shape, F32)
    acc_sc[...] = jnp.zeros(acc_sc.shape, F32)

    def chunks(specs, shifted):
        kcs = [k_ref[pl.ds(pl.multiple_of(j * t, t), t), :] for j, _ in specs]
        vts = [vt_ref[j] for j, _ in specs]
        units = [(c, hd) for c in range(len(specs)) for hd in range(N_HEADS)]
        if any(masked for _, masked in specs):
            key = lax.broadcasted_iota(jnp.int32, (t, t), 0)
            qry = lax.broadcasted_iota(jnp.int32, (t, t), 1)
            keep = key <= qry
        ahead = 2
        score = lambda unit: _dot_nt(kcs[unit[0]], q_ref[unit[1]])
        pending = [score(unit) for unit in units[:ahead]]
        for u, (c, hd) in enumerate(units):
            st = pending.pop(0)
            if u + ahead < len(units):
                pending.append(score(units[u + ahead]))
            if specs[c][1]:
                st = jnp.where(keep, st, NEG_INF)
            if shifted:
                p = jnp.exp2(st)
                l_sc[hd] += jnp.sum(p, axis=0, keepdims=True)
                acc_sc[hd] += _dot(vts[c], p.astype(BF16))
            else:
                m_prev = m_sc[hd]
                m_new = jnp.maximum(m_prev, jnp.max(st, axis=0, keepdims=True))
                alpha = jnp.exp2(m_prev - m_new)
                p = jnp.exp2(st - m_new)
                l_sc[hd] = alpha * l_sc[hd] + jnp.sum(p, axis=0, keepdims=True)
                acc_sc[hd] = alpha * acc_sc[hd] + _dot(vts[c], p.astype(BF16))
                m_sc[hd] = m_new

    def sweep(shifted, width):
        def body(jw, carry):
            chunks([(jw * width + c, False) for c in range(width)], shifted)
            return carry
        lax.fori_loop(0, i // width, body, 0)
        if width == 1:
            chunks([(i, True)], shifted)
        else:
            assert width == 2
            odd = i % 2 == 1

            @pl.when(odd)
            def _():
                chunks([(i - 1, False), (i, True)], shifted)

            @pl.when(jnp.logical_not(odd))
            def _():
                chunks([(i, True)], shifted)

    safe = safe_ref[i] != 0

    @pl.when(safe)
    def _():
        sweep(True, 2)

    @pl.when(jnp.logical_not(safe))
    def _():
        m_sc[...] = jnp.full(m_sc.shape, NEG_INF, F32)
        sweep(False, 1)

    for hd in range(N_HEADS):
        o = acc_sc[hd] / l_sc[hd]
        o_ref[:, hd * KV_RANK:(hd + 1) * KV_RANK] = o.T.astype(BF16)


def _epilogue_kernel(o_ref, sz_ref, ga_ref, yp_ref, x_ref, w_uv_ref, w_oatt_ref, w_out_ref,
                     g_final_ref, y_ref):
    o = o_ref[...]
    o2 = jnp.concatenate(
        [_dot(o[:, hd * KV_RANK:(hd + 1) * KV_RANK], w_uv_ref[hd]) for hd in range(N_HEADS)], axis=1)
    y_att = _dot((o2 * sz_ref[...].astype(F32)).astype(BF16), w_oatt_ref[...])
    merged = ga_ref[...].astype(F32) * y_att + yp_ref[...].astype(F32)
    xo = x_ref[...] + _dot(merged.astype(BF16), w_out_ref[...])
    y_ref[...] = _rms(xo, g_final_ref[...])


def _sample_proj_kernel(x_ref, cs_ref, hist_ref, g_pre_ref, w_a_ref, w_za_ref, w_up_ref,
                        w_zp_ref, w_ga_ref, w_gp_ref, g_q_ref, w_qn_ref, w_qr_ref, w_ukt_ref,
                        g_kv_ref, w_pool_ref, pscale_ref, w_opool_ref,
                        qa_ref, qr_ref, ckv_ref, kr_ref, sz_ref, ga_ref, yp_ref, pstate_ref,
                        *, past_len):
    h = _rms(x_ref[...], g_pre_ref[...]).astype(BF16)
    za = _dot_nt(h, w_a_ref[...])
    qa, qr, ckv, kr2 = _queries_keys(za, cs_ref[...], g_q_ref[...], w_qn_ref[...], w_qr_ref[...],
                                     w_ukt_ref, g_kv_ref[...])
    for hd in range(N_HEADS):
        qa_ref[:, hd * LANES:(hd + 1) * LANES] = qa[hd]
        qr_ref[:, hd * LANES:(hd + 1) * LANES] = qr[hd]
    ckv_ref[...] = ckv
    kr_ref[...] = kr2[:, :QK_ROPE]

    u = _gates_and_pool_input(h, w_up_ref, w_za_ref, w_ga_ref, sz_ref, ga_ref)
    win_sum, rcnt = [], []
    for g, w in enumerate(POOL_WINDOWS):
        c0 = g * POOL_GROUP_DIM
        acc = u[:, c0:c0 + POOL_GROUP_DIM]
        for s in range(1, w):
            acc = acc + hist_ref[POOL_HIST - s, :, c0:c0 + POOL_GROUP_DIM]
        win_sum.append(acc)
        rcnt.append(1.0 / float(min(w, past_len + 1)))
    yp_ref[...] = _pool_branch(h, win_sum, u, rcnt, w_zp_ref[...], w_gp_ref[...], w_pool_ref,
                               pscale_ref[...], w_opool_ref[...]).astype(yp_ref.dtype)
    pstate_ref[0:POOL_HIST - 1] = hist_ref[1:POOL_HIST]
    pstate_ref[POOL_HIST - 1] = u


def _lane_group_reduce(v, op):
    sh = N_HEADS
    while sh < LANES:
        v = op(v, pltpu.roll(v, sh, axis=1))
        sh *= 2
    return v


def _diag_to_col(v):
    sub = lax.broadcasted_iota(jnp.int32, v.shape, 0)
    lane = lax.broadcasted_iota(jnp.int32, v.shape, 1)
    return jnp.sum(jnp.where(sub == lane, v, 0.0), axis=1, keepdims=True)


def _latent_rows(kbuf):
    n_pairs, _, page, width = kbuf.shape
    halves = []
    for half in range(2):
        cols = []
        for jj in range(KEYS_PER_ROW // 2):
            r0 = (half * (KEYS_PER_ROW // 2) + jj) * 8
            for par in range(2):
                cols.append(kbuf[:, par, r0:r0 + 8, :].reshape(n_pairs * 8, width))
        halves.append(jnp.concatenate(cols, axis=1))
    return jnp.concatenate(halves, axis=0).astype(BF16)


def _rotary_rows(rbuf):
    n_pairs = rbuf.shape[0]
    xt = jnp.swapaxes(rbuf[...], 1, 2)
    halves = []
    for half in range(2):
        cols = []
        for jj in range(KEYS_PER_ROW // 2):
            r0 = (half * (KEYS_PER_ROW // 2) + jj) * 8
            cols.append(xt[:, r0:r0 + 8, :].reshape(n_pairs * 8, LANES))
        halves.append(jnp.concatenate(cols, axis=1))
    return jnp.concatenate(halves, axis=0).astype(BF16)


def _attend_one(qat, qrt, knew, rnew, kbuf, rbuf):
    lane_a = lax.broadcasted_iota(jnp.int32, qat.shape, 1) // N_HEADS
    lane_r = lax.broadcasted_iota(jnp.int32, qrt.shape, 1) // N_HEADS
    zero = jnp.zeros((), BF16)
    wk = jnp.concatenate([jnp.where(lane_a == j, qat, zero) for j in range(KEYS_PER_ROW)], axis=0)
    wr = jnp.concatenate([jnp.where(lane_r == j, qrt, zero) for j in range(KEYS_PER_ROW)], axis=0)

    kb = _latent_rows(kbuf)
    rb = _rotary_rows(rbuf)
    hr = kb.shape[0] // 2
    s_lat = [_dot(kb[:hr], wk), _dot(kb[hr:], wk)]
    s_rot = [_dot(rb[:hr], wr), _dot(rb[hr:], wr)]
    s2 = jnp.concatenate([s_lat[0] + s_rot[0], s_lat[1] + s_rot[1]], axis=0)

    knew_b = knew.astype(BF16)
    rnew_b = rnew.astype(BF16)
    s_new = (_dot(jnp.broadcast_to(knew_b, (8, KV_RANK)), qat)
             + _dot(jnp.broadcast_to(rnew_b, (8, QK_ROPE)), qrt))

    m = jnp.max(s2, axis=0, keepdims=True)
    m = _lane_group_reduce(jnp.broadcast_to(m, (8, LANES)), jnp.maximum)
    m = jnp.maximum(m, s_new)
    p2 = jnp.exp2(s2 - m[0:1])
    p_new = jnp.exp2(s_new - m)
    l_lane = _lane_group_reduce(
        jnp.broadcast_to(jnp.sum(p2, axis=0, keepdims=True), (8, LANES)), jnp.add)
    p_new_col = _diag_to_col(p_new)
    l_col = _diag_to_col(l_lane) + p_new_col

    p2t = p2.T.astype(BF16)
    hs = KEYS_PER_ROW // 2
    hp, hk = hs * N_HEADS, hs * KV_RANK
    g_halves = [_dot(p2t[:hp], kb[:, :hk]), _dot(p2t[hp:], kb[:, hk:])]
    o = p_new_col.astype(BF16).astype(F32) * knew_b.astype(F32)
    for g in g_halves:
        for j in range(hs):
            o = o + g[j * N_HEADS:(j + 1) * N_HEADS, j * KV_RANK:(j + 1) * KV_RANK]
    return o / l_col


def _sample_attn_kernel(pt_ref, qat_ref, qrt_ref, knew_ref, rnew_ref, ck_hbm, krt_hbm, o_ref,
                        kbuf_a, rbuf_a, kbuf_b, rbuf_b, sems, *, n_pages):
    g = pl.program_id(0)
    ng = pl.num_programs(0)
    bufs = ((kbuf_a, rbuf_a), (kbuf_b, rbuf_b))

    def start_pair(seq, s, q):
        kbuf, rbuf = bufs[s]
        for par in range(2):
            pid = pt_ref[seq * n_pages + 2 * q + par]
            pltpu.make_async_copy(ck_hbm.at[pid], kbuf.at[q, par], sems.at[0, s]).start(priority=par)
            pltpu.make_async_copy(krt_hbm.at[pid], rbuf.at[q, par * QK_ROPE:(par + 1) * QK_ROPE, :],
                                  sems.at[1, s]).start(priority=1 - par)

    def start_inline(seq, s):
        for q in range(n_pages // 2):
            start_pair(seq, s, q)

    def wait_all(s):
        kbuf, rbuf = bufs[s]
        pltpu.make_async_copy(kbuf, kbuf, sems.at[0, s]).wait()
        pltpu.make_async_copy(rbuf, rbuf, sems.at[1, s]).wait()

    @pl.when(g == 0)
    def _():
        def body(q, c):
            start_pair(0, 0, q)
            start_pair(1, 1, q)
            return c
        lax.fori_loop(0, n_pages // 2, body, 0)

    nxt = jnp.minimum(g + 1, ng - 1)
    wait_all(0)
    o_ref[0] = _attend_one(qat_ref[0], qrt_ref[0], knew_ref[0], rnew_ref[0], kbuf_a, rbuf_a)
    start_inline(2 * nxt, 0)
    wait_all(1)
    o_ref[1] = _attend_one(qat_ref[1], qrt_ref[1], knew_ref[1], rnew_ref[1], kbuf_b, rbuf_b)
    start_inline(2 * nxt + 1, 1)

    @pl.when(g == ng - 1)
    def _():
        wait_all(0)
        wait_all(1)


def _const_spec(shape):
    zeros = (0,) * len(shape)
    return pl.BlockSpec(shape, lambda *_: zeros)


def _prep_weights(g_pre, w_in, g_q, w_uq, g_kv, w_uk, w_uv, w_o_att, w_pool, pool_scale,
                  w_o_pool, w_out, g_final):
    d_model = w_in.shape[0]
    att_w = N_HEADS * V_DIM
    half = QK_ROPE // 2
    o_kv = Q_RANK
    o_kr = o_kv + KV_RANK
    o_za = o_kr + QK_ROPE
    o_up = o_za + att_w
    o_zp = o_up + POOL_WIDTH
    o_ga = o_zp + POOL_WIDTH
    o_gp = o_ga + d_model
    w_t = w_in.T
    w_kr = w_t[o_kr:o_za]
    w_a = jnp.concatenate([w_t[:o_kr], w_kr, -w_kr[half:], w_kr[:half]], axis=0)
    uq = w_uq.reshape(Q_RANK, N_HEADS, QK_NOPE + QK_ROPE)
    w_qn = uq[:, :, :QK_NOPE].reshape(Q_RANK, N_HEADS * QK_NOPE)
    x1, x2 = uq[:, :, QK_NOPE:QK_NOPE + half], uq[:, :, QK_NOPE + half:]
    w_qr = jnp.concatenate([x1, x2, -x2, x1], axis=2).reshape(Q_RANK, N_HEADS * LANES)
    bf = lambda a: a.astype(BF16)
    row = lambda a: a.reshape(1, -1).astype(F32)
    return dict(
        g_pre=row(g_pre), w_a=bf(w_a), w_za=bf(w_t[o_za:o_up]), w_up=bf(w_t[o_up:o_zp]),
        w_zp=bf(w_t[o_zp:o_ga]), w_ga=bf(w_t[o_ga:o_gp]), w_gp=bf(w_t[o_gp:]),
        g_q=row(g_q), w_qn=bf(w_qn), w_qr=bf(w_qr),
        w_ukt=bf(jnp.transpose(w_uk, (1, 2, 0))),
        g_kv=row(g_kv), w_pool=bf(w_pool), pscale=row(pool_scale), w_opool=bf(w_o_pool),
        w_uv=bf(jnp.transpose(w_uv, (1, 0, 2))),
        w_oatt=bf(w_o_att), w_out=bf(w_out), g_final=row(g_final))


_PROJ_WEIGHTS = ("g_pre", "w_a", "w_za", "w_up", "w_zp", "w_ga", "w_gp", "g_q", "w_qn", "w_qr",
                 "w_ukt", "g_kv", "w_pool", "pscale", "w_opool")


def _cos_sin(pos):
    half = QK_ROPE // 2
    inv = ROPE_BASE ** (-jnp.arange(half, dtype=F32) / half)
    ang = pos.astype(F32)[:, None] * inv[None, :]
    return jnp.cos(ang), jnp.sin(ang)


def _rope_table(pos):
    c, s = _cos_sin(pos)
    return jnp.concatenate([c, c, s, s], axis=1)


ROPE_BLOCK = 128


def _rope_factors(seq):
    ca, sa = _cos_sin(jnp.arange(0, seq, ROPE_BLOCK, dtype=jnp.int32))
    cb, sb = _cos_sin(jnp.arange(ROPE_BLOCK, dtype=jnp.int32))
    rope_a = jnp.stack([jnp.concatenate([ca, ca, sa, sa], axis=1),
                        jnp.concatenate([-sa, -sa, ca, ca], axis=1)], axis=1)
    rope_b = jnp.stack([jnp.concatenate([cb] * 4, axis=1),
                        jnp.concatenate([sb] * 4, axis=1)], axis=0)
    return rope_a, rope_b


def _prompt_proj(x, rope_a, rope_b, hist, wts, tm, t_attn):
    seq, d_model = x.shape
    n = seq // tm
    per_attn = t_attn // tm
    blocks = tm // ROPE_BLOCK
    hist_rows = hist.shape[0]
    ws = [wts[k] for k in _PROJ_WEIGHTS]
    rows = lambda width: pl.BlockSpec((tm, width), lambda i: (i, 0))
    out_shape = (
        jax.ShapeDtypeStruct((N_HEADS, seq, QCAT), BF16),
        jax.ShapeDtypeStruct((seq, QCAT), BF16),
        jax.ShapeDtypeStruct((seq // t_attn, KV_RANK, t_attn), BF16),
        jax.ShapeDtypeStruct((seq, KV_RANK), F32),
        jax.ShapeDtypeStruct((QK_ROPE, seq), F32),
        jax.ShapeDtypeStruct((seq, d_model), BF16),
        jax.ShapeDtypeStruct((seq, d_model), BF16),
        jax.ShapeDtypeStruct((seq, d_model), BF16),
        jax.ShapeDtypeStruct((hist_rows, POOL_WIDTH), F32),
        jax.ShapeDtypeStruct((n, 8, LANES), F32),
    )
    out_specs = (
        pl.BlockSpec((N_HEADS, tm, QCAT), lambda i: (0, i, 0)),
        rows(QCAT),
        pl.BlockSpec((1, KV_RANK, tm), lambda i: (i // per_attn, 0, i % per_attn)),
        rows(KV_RANK), pl.BlockSpec((QK_ROPE, tm), lambda i: (0, i)),
        rows(d_model), rows(d_model), rows(d_model),
        _const_spec((hist_rows, POOL_WIDTH)),
        pl.BlockSpec((1, 8, LANES), lambda i: (i, 0, 0)),
    )
    return pl.pallas_call(
        functools.partial(_prompt_proj_kernel, tm=tm),
        grid=(n,),
        in_specs=[rows(d_model), pl.BlockSpec((blocks, 2, LANES), lambda i: (i, 0, 0)),
                  _const_spec(rope_b.shape), _const_spec(hist.shape)]
        + [_const_spec(w.shape) for w in ws],
        out_specs=out_specs,
        out_shape=out_shape,
        scratch_shapes=[pltpu.VMEM((hist_rows + tm, POOL_WIDTH), F32),
                        pltpu.VMEM((8, LANES), F32)],
        compiler_params=pltpu.CompilerParams(
            dimension_semantics=("arbitrary",), vmem_limit_bytes=VMEM_LIMIT),
    )(x, rope_a, rope_b, hist, *ws)


def _prompt_attn(safe, qcat, kcat, vt, t):
    seq = kcat.shape[0]
    n = seq // t
    grid_spec = pltpu.PrefetchScalarGridSpec(
        num_scalar_prefetch=1,
        grid=(n,),
        in_specs=[pl.BlockSpec((N_HEADS, t, QCAT), lambda i, s: (0, i, 0)),
                  pl.BlockSpec(memory_space=pltpu.VMEM),
                  pl.BlockSpec(memory_space=pltpu.VMEM)],
        out_specs=pl.BlockSpec((t, N_HEADS * KV_RANK), lambda i, s: (i, 0)),
        scratch_shapes=[pltpu.VMEM((N_HEADS, 1, t), F32),
                        pltpu.VMEM((N_HEADS, 1, t), F32),
                        pltpu.VMEM((N_HEADS, KV_RANK, t), F32)],
    )
    return pl.pallas_call(
        functools.partial(_prompt_attn_kernel, t=t),
        grid_spec=grid_spec,
        out_shape=jax.ShapeDtypeStruct((seq, N_HEADS * KV_RANK), BF16),
        compiler_params=pltpu.CompilerParams(
            dimension_semantics=("arbitrary",), vmem_limit_bytes=VMEM_LIMIT),
    )(safe, qcat, kcat, vt)


def _epilogue(o, sz, ga, yp, x, wts, tm):
    n_rows, d_model = x.shape
    rows = pl.BlockSpec((tm, d_model), lambda i: (i, 0))
    ws = [wts[k] for k in ("w_uv", "w_oatt", "w_out", "g_final")]
    return pl.pallas_call(
        _epilogue_kernel,
        grid=(n_rows // tm,),
        in_specs=[rows] * 5 + [_const_spec(w.shape) for w in ws],
        out_specs=rows,
        out_shape=jax.ShapeDtypeStruct((n_rows, d_model), F32),
        compiler_params=pltpu.CompilerParams(
            dimension_semantics=("arbitrary",), vmem_limit_bytes=VMEM_LIMIT),
    )(o, sz, ga, yp, x, *ws)


def _sample_proj(x, cs, hist, wts, past_len):
    b, d_model = x.shape
    ws = [wts[k] for k in _PROJ_WEIGHTS]
    wide = jax.ShapeDtypeStruct((b, d_model), BF16)
    out_shape = (
        jax.ShapeDtypeStruct((b, N_HEADS * LANES), F32),
        jax.ShapeDtypeStruct((b, N_HEADS * LANES), F32),
        jax.ShapeDtypeStruct((b, KV_RANK), F32),
        jax.ShapeDtypeStruct((b, QK_ROPE), F32),
        wide, wide, wide,
        jax.ShapeDtypeStruct(hist.shape, F32),
    )
    args = (x, cs, hist, *ws)
    return pl.pallas_call(
        functools.partial(_sample_proj_kernel, past_len=past_len),
        grid=(1,),
        in_specs=[_const_spec(a.shape) for a in args],
        out_specs=tuple(_const_spec(s.shape) for s in out_shape),
        out_shape=out_shape,
        compiler_params=pltpu.CompilerParams(
            dimension_semantics=("arbitrary",), vmem_limit_bytes=VMEM_LIMIT),
    )(*args)


def _sample_attn(page_table, qat, qrt, knew, rnew, cache_ckv, cache_krope):
    b, n_pages = page_table.shape
    page = cache_ckv.shape[1]
    assert b % 2 == 0 and n_pages % 2 == 0
    pair = lambda rows, width: pl.BlockSpec((2, rows, width), lambda i, pt: (i, 0, 0))
    kbuf = pltpu.VMEM((n_pages // 2, 2, page, KV_RANK), F32)
    rbuf = pltpu.VMEM((n_pages // 2, 2 * QK_ROPE, page), F32)
    grid_spec = pltpu.PrefetchScalarGridSpec(
        num_scalar_prefetch=1,
        grid=(b // 2,),
        in_specs=[pair(KV_RANK, LANES), pair(QK_ROPE, LANES), pair(1, KV_RANK), pair(1, QK_ROPE),
                  pl.BlockSpec(memory_space=pl.ANY),
                  pl.BlockSpec(memory_space=pl.ANY)],
        out_specs=pair(N_HEADS, KV_RANK),
        scratch_shapes=[kbuf, rbuf, kbuf, rbuf, pltpu.SemaphoreType.DMA((2, 2))],
    )
    return pl.pallas_call(
        functools.partial(_sample_attn_kernel, n_pages=n_pages),
        grid_spec=grid_spec,
        out_shape=jax.ShapeDtypeStruct((b, N_HEADS, KV_RANK), F32),
        compiler_params=pltpu.CompilerParams(
            dimension_semantics=("arbitrary",), vmem_limit_bytes=VMEM_LIMIT),
    )(page_table.reshape(-1), qat, qrt, knew, rnew, cache_ckv, jnp.swapaxes(cache_krope, 1, 2))


def kernel(x_prompt, x_sample, cache_ckv, cache_krope, state_pool, page_table, g_pre, w_in, g_q,
           w_uq, g_kv, w_uk, w_uv, w_o_att, w_pool, pool_scale, w_o_pool, w_out, g_final):
    depth = g_pre.shape[0]
    batch, seq, d_model = x_prompt.shape
    dec_b, dec_seq, _ = x_sample.shape
    assert depth == 1 and batch == 1 and dec_seq == 1
    past_len = page_table.shape[1] * cache_ckv.shape[2]
    wts = _prep_weights(g_pre[0], w_in[0], g_q[0], w_uq[0], g_kv[0], w_uk[0], w_uv[0], w_o_att[0],
                        w_pool[0], pool_scale[0], w_o_pool[0], w_out[0], g_final)

    tile = 512
    proj_tile = 512
    hist_rows = 16
    xp = x_prompt.reshape(seq, d_model)
    rope_a, rope_b = _rope_factors(seq)
    hist0 = jnp.zeros((hist_rows, POOL_WIDTH), F32)
    qcat, kcat, vt, ckv_p, krt_p, sz, ga, yp, pstate, slack = _prompt_proj(
        xp, rope_a, rope_b, hist0, wts, proj_tile, tile)
    slack = jnp.max(slack[:, 0, 0].reshape(seq // tile, tile // proj_tile), axis=1)
    safe = (slack <= SAFE_SLACK).astype(jnp.int32)
    o_p = _prompt_attn(safe, qcat, kcat, vt, tile)
    y_p = _epilogue(o_p, sz, ga, yp, xp, wts, tile)

    xs = x_sample.reshape(dec_b, d_model)
    cs_s = _rope_table(jnp.full((dec_b,), past_len, jnp.int32))
    hist_s = jnp.transpose(state_pool[0], (1, 0, 2))
    qa, qr, ckv_s, kr_s, sz_s, ga_s, yp_s, pstate_s = _sample_proj(xs, cs_s, hist_s, wts, past_len)
    qa3 = qa.reshape(dec_b, N_HEADS, LANES)
    qr3 = qr.reshape(dec_b, N_HEADS, LANES)[:, :, :QK_ROPE]
    qat = jnp.tile(jnp.transpose(qa3, (0, 2, 1)), (1, 1, KEYS_PER_ROW)).astype(BF16)
    qrt = jnp.tile(jnp.transpose(qr3, (0, 2, 1)), (1, 1, KEYS_PER_ROW)).astype(BF16)
    o_s = _sample_attn(page_table, qat, qrt, ckv_s.reshape(dec_b, 1, KV_RANK),
                       kr_s.reshape(dec_b, 1, QK_ROPE), cache_ckv[0], cache_krope[0])
    o_s = o_s.reshape(dec_b, N_HEADS * KV_RANK).astype(BF16)
    y_s = _epilogue(o_s, sz_s, ga_s, yp_s, xs, wts, dec_b)

    return (y_p.reshape(batch, seq, d_model),
            y_s.reshape(dec_b, 1, d_model),
            ckv_p.reshape(1, batch, seq, KV_RANK),
            krt_p.T.reshape(1, batch, seq, QK_ROPE),
            pstate[hist_rows - POOL_HIST:].reshape(1, batch, POOL_HIST, POOL_WIDTH),
            ckv_s.reshape(1, dec_b, 1, KV_RANK),
            kr_s.reshape(1, dec_b, 1, QK_ROPE),
            jnp.transpose(pstate_s, (1, 0, 2)).reshape(1, dec_b, POOL_HIST, POOL_WIDTH))
```

```python
import functools

import jax
import jax.numpy as jnp
from jax import lax
from jax.experimental import pallas as pl
from jax.experimental.pallas import tpu as pltpu

F32 = jnp.float32
BF16 = jnp.bfloat16

N_HEADS = 8
QK_NOPE = 128
QK_ROPE = 64
V_DIM = 128
Q_RANK = 256
KV_RANK = 128
ROPE_BASE = 10000.0
POOL_WINDOWS = (2, 4, 8, 16)
POOL_GROUP_DIM = 128
POOL_WIDTH = len(POOL_WINDOWS) * POOL_GROUP_DIM
POOL_HIST = 15
EPS = 1e-6
NEG_INF = -1e30
SOFTMAX_SCALE = (QK_NOPE + QK_ROPE) ** -0.5
LOG2_E = 1.4426950408889634
Q_SCALE = SOFTMAX_SCALE * LOG2_E
SAFE_SLACK = 64.0

LANES = 128
KEYS_PER_ROW = LANES // N_HEADS
QCAT = 2 * LANES
VMEM_LIMIT = 56 * 1024 * 1024


def _dot(a, b):
    return jnp.dot(a, b, preferred_element_type=F32)


def _dot_nt(a, b):
    return lax.dot_general(a, b, (((1,), (1,)), ((), ())), preferred_element_type=F32)


def _rms(x, g):
    return x * lax.rsqrt(jnp.mean(x * x, axis=-1, keepdims=True) + EPS) * g


def _silu(x):
    return x * jax.nn.sigmoid(x)


def _rope_pair(t):
    return t + pltpu.roll(t, QK_ROPE, axis=1)


def _queries_keys(za, cs, g_q, w_qn, w_qr, w_ukt, g_kv):
    qn = _rms(za[:, :Q_RANK], g_q).astype(BF16)
    ckv = _rms(za[:, Q_RANK:Q_RANK + KV_RANK], g_kv)
    kr2 = _rope_pair(za[:, Q_RANK + KV_RANK:] * cs)
    q_nope = _dot(qn, w_qn)
    q_rope = _dot(qn, w_qr)
    qa, qr = [], []
    for hd in range(N_HEADS):
        sl = slice(hd * LANES, (hd + 1) * LANES)
        qa.append(_dot(q_nope[:, sl].astype(BF16), w_ukt[hd]) * Q_SCALE)
        qr.append(_rope_pair(q_rope[:, sl] * cs) * Q_SCALE)
    return qa, qr, ckv, kr2


def _pool_branch(h, win_sum, u, rcnt, w_zp, w_gp, w_pool, pool_scale, w_o_pool):
    mixed = []
    for g in range(len(POOL_WINDOWS)):
        sl = slice(g * POOL_GROUP_DIM, (g + 1) * POOL_GROUP_DIM)
        pooled = win_sum[g] * rcnt[g] - u[:, sl]
        mixed.append(_dot(pooled.astype(BF16), w_pool[g]))
    p = jnp.concatenate(mixed, axis=1) * pool_scale
    z_pool = _dot_nt(h, w_zp)
    y_pool = _dot((p * _silu(z_pool)).astype(BF16), w_o_pool)
    return jax.nn.sigmoid(_dot_nt(h, w_gp)) * y_pool


def _gates_and_pool_input(h, w_up_ref, w_za_ref, w_ga_ref, sz_ref, ga_ref):
    sz_ref[...] = _silu(_dot_nt(h, w_za_ref[...])).astype(sz_ref.dtype)
    ga_ref[...] = jax.nn.sigmoid(_dot_nt(h, w_ga_ref[...])).astype(ga_ref.dtype)
    return _dot_nt(h, w_up_ref[...])


def _prompt_proj_kernel(x_ref, rope_a_ref, rope_b_ref, hist_ref, g_pre_ref, w_a_ref, w_za_ref, w_up_ref,
                        w_zp_ref, w_ga_ref, w_gp_ref, g_q_ref, w_qn_ref, w_qr_ref, w_ukt_ref,
                        g_kv_ref, w_pool_ref, pscale_ref, w_opool_ref,
                        qcat_ref, kcat_ref, vt_ref, ckv_ref, kr_ref, sz_ref, ga_ref, yp_ref,
                        pstate_ref, slack_ref, uext_ref, kmax_ref, *, tm):
    i = pl.program_id(0)
    hist_rows = uext_ref.shape[0] - tm

    @pl.when(i == 0)
    def _():
        uext_ref[0:hist_rows, :] = hist_ref[...]
        kmax_ref[...] = jnp.zeros(kmax_ref.shape, F32)

    cs = jnp.concatenate(
        [rope_a_ref[a, 0:1, :] * rope_b_ref[0] + rope_a_ref[a, 1:2, :] * rope_b_ref[1]
         for a in range(rope_a_ref.shape[0])], axis=0)
    h = _rms(x_ref[...], g_pre_ref[...]).astype(BF16)
    za = _dot_nt(h, w_a_ref[...])
    qa, qr, ckv, kr2 = _queries_keys(za, cs, g_q_ref[...], w_qn_ref[...], w_qr_ref[...], w_ukt_ref,
                                     g_kv_ref[...])
    lane = lax.broadcasted_iota(jnp.int32, kr2.shape, 1)
    is_rope = lane < QK_ROPE
    is_shift = lane == QK_ROPE
    rsum = lambda a: jnp.sum(a, axis=1, keepdims=True)
    ckv_b = ckv.astype(BF16).astype(F32)
    kr_b = jnp.where(is_rope, kr2, 0.0).astype(BF16).astype(F32)
    k_norm2 = jnp.max(rsum(ckv_b * ckv_b + kr_b * kr_b), axis=0, keepdims=True)
    k_norm2 = jnp.maximum(k_norm2, kmax_ref[0:1, 0:1])
    kmax_ref[...] = jnp.broadcast_to(k_norm2, kmax_ref.shape)
    slack = None
    for hd in range(N_HEADS):
        qa_b = qa[hd].astype(BF16)
        qr_b = jnp.where(is_rope, qr[hd], 0.0).astype(BF16).astype(F32)
        qa_f = qa_b.astype(F32)
        c = jnp.sqrt(rsum(qa_f * qa_f + qr_b * qr_b) * k_norm2) + 1.0
        c = c.astype(BF16).astype(F32)
        own = rsum(qa_f * ckv_b + qr_b * kr_b)
        worst = jnp.max(c - own, axis=0, keepdims=True)
        slack = worst if slack is None else jnp.maximum(slack, worst)
        qcat_ref[hd, :, 0:LANES] = qa_b
        qcat_ref[hd, :, LANES:QCAT] = jnp.where(is_shift, -c, qr_b).astype(BF16)
    slack_ref[0] = jnp.broadcast_to(slack, slack_ref.shape[1:])
    kcat_ref[:, 0:LANES] = ckv.astype(BF16)
    kcat_ref[:, LANES:QCAT] = jnp.where(is_shift, 1.0, kr_b).astype(BF16)
    vt_ref[0] = ckv.T.astype(BF16)
    ckv_ref[...] = ckv
    kr_ref[...] = kr2.T[:QK_ROPE, :]

    u = _gates_and_pool_input(h, w_up_ref, w_za_ref, w_ga_ref, sz_ref, ga_ref)
    uext_ref[hist_rows:, :] = u
    pos = i * tm + lax.broadcasted_iota(jnp.int32, (tm, 1), 0)
    win_sum, rcnt = [], []
    for g, w in enumerate(POOL_WINDOWS):
        c0 = g * POOL_GROUP_DIM
        acc = u[:, c0:c0 + POOL_GROUP_DIM]
        for s in range(1, w):
            acc = acc + uext_ref[hist_rows - s:hist_rows - s + tm, c0:c0 + POOL_GROUP_DIM]
        win_sum.append(acc)
        rcnt.append(1.0 / jnp.minimum(w, pos + 1).astype(F32))
    yp_ref[...] = _pool_branch(h, win_sum, u, rcnt, w_zp_ref[...], w_gp_ref[...], w_pool_ref,
                               pscale_ref[...], w_opool_ref[...]).astype(yp_ref.dtype)
    tail = uext_ref[tm:tm + hist_rows, :]
    pstate_ref[...] = tail
    uext_ref[0:hist_rows, :] = tail


def _prompt_attn_kernel(safe_ref, q_ref, k_ref, vt_ref, o_ref, m_sc, l_sc, acc_sc, *, t):
    i = pl.program_id(0)
    l_sc[...] = jnp.zeros(l_sc.shape, F32)
    acc_sc[...] = jnp.zeros(acc_sc.shape, F32)

    def chunks(specs, shifted):
        kcs = [k_ref[pl.ds(pl.multiple_of(j * t, t), t), :] for j, _ in specs]
        vts = [vt_ref[j] for j, _ in specs]
        units = [(c, hd) for c in range(len(specs)) for hd in range(N_HEADS)]
        if any(masked for _, masked in specs):
            key = lax.broadcasted_iota(jnp.int32, (t, t), 0)
            qry = lax.broadcasted_iota(jnp.int32, (t, t), 1)
            keep = key <= qry
        scores = [_dot_nt(kcs[c], q_ref[hd]) for c, hd in units]
        for (c, hd), st in zip(units, scores):
            if specs[c][1]:
                st = jnp.where(keep, st, NEG_INF)
            if shifted:
                p = jnp.exp2(st)
                l_sc[hd] += jnp.sum(p, axis=0, keepdims=True)
                acc_sc[hd] += _dot(vts[c], p.astype(BF16))
            else:
                m_prev = m_sc[hd]
                m_new = jnp.maximum(m_prev, jnp.max(st, axis=0, keepdims=True))
                alpha = jnp.exp2(m_prev - m_new)
                p = jnp.exp2(st - m_new)
                l_sc[hd] = alpha * l_sc[hd] + jnp.sum(p, axis=0, keepdims=True)
                acc_sc[hd] = alpha * acc_sc[hd] + _dot(vts[c], p.astype(BF16))
                m_sc[hd] = m_new

    def sweep(shifted, width):
        def body(jw, carry):
            chunks([(jw * width + c, False) for c in range(width)], shifted)
            return carry
        lax.fori_loop(0, i // width, body, 0)
        if width == 1:
            chunks([(i, True)], shifted)
        else:
            assert width == 2
            odd = i % 2 == 1

            @pl.when(odd)
            def _():
                chunks([(i - 1, False), (i, True)], shifted)

            @pl.when(jnp.logical_not(odd))
            def _():
                chunks([(i, True)], shifted)

    safe = safe_ref[i] != 0

    @pl.when(safe)
    def _():
        sweep(True, 2)

    @pl.when(jnp.logical_not(safe))
    def _():
        m_sc[...] = jnp.full(m_sc.shape, NEG_INF, F32)
        sweep(False, 1)

    for hd in range(N_HEADS):
        o = acc_sc[hd] / l_sc[hd]
        o_ref[:, hd * KV_RANK:(hd + 1) * KV_RANK] = o.T.astype(BF16)


def _epilogue_kernel(o_ref, sz_ref, ga_ref, yp_ref, x_ref, w_uv_ref, w_oatt_ref, w_out_ref,
                     g_final_ref, y_ref):
    o = o_ref[...]
    o2 = jnp.concatenate(
        [_dot(o[:, hd * KV_RANK:(hd + 1) * KV_RANK], w_uv_ref[hd]) for hd in range(N_HEADS)], axis=1)
    y_att = _dot((o2 * sz_ref[...].astype(F32)).astype(BF16), w_oatt_ref[...])
    merged = ga_ref[...].astype(F32) * y_att + yp_ref[...].astype(F32)
    xo = x_ref[...] + _dot(merged.astype(BF16), w_out_ref[...])
    y_ref[...] = _rms(xo, g_final_ref[...])


def _sample_proj_kernel(x_ref, cs_ref, hist_ref, g_pre_ref, w_a_ref, w_za_ref, w_up_ref,
                        w_zp_ref, w_ga_ref, w_gp_ref, g_q_ref, w_qn_ref, w_qr_ref, w_ukt_ref,
                        g_kv_ref, w_pool_ref, pscale_ref, w_opool_ref,
                        qa_ref, qr_ref, ckv_ref, kr_ref, sz_ref, ga_ref, yp_ref, pstate_ref,
                        *, past_len):
    h = _rms(x_ref[...], g_pre_ref[...]).astype(BF16)
    za = _dot_nt(h, w_a_ref[...])
    qa, qr, ckv, kr2 = _queries_keys(za, cs_ref[...], g_q_ref[...], w_qn_ref[...], w_qr_ref[...],
                                     w_ukt_ref, g_kv_ref[...])
    for hd in range(N_HEADS):
        qa_ref[:, hd * LANES:(hd + 1) * LANES] = qa[hd]
        qr_ref[:, hd * LANES:(hd + 1) * LANES] = qr[hd]
    ckv_ref[...] = ckv
    kr_ref[...] = kr2[:, :QK_ROPE]

    u = _gates_and_pool_input(h, w_up_ref, w_za_ref, w_ga_ref, sz_ref, ga_ref)
    win_sum, rcnt = [], []
    for g, w in enumerate(POOL_WINDOWS):
        c0 = g * POOL_GROUP_DIM
        acc = u[:, c0:c0 + POOL_GROUP_DIM]
        for s in range(1, w):
            acc = acc + hist_ref[POOL_HIST - s, :, c0:c0 + POOL_GROUP_DIM]
        win_sum.append(acc)
        rcnt.append(1.0 / float(min(w, past_len + 1)))
    yp_ref[...] = _pool_branch(h, win_sum, u, rcnt, w_zp_ref[...], w_gp_ref[...], w_pool_ref,
                               pscale_ref[...], w_opool_ref[...]).astype(yp_ref.dtype)
    pstate_ref[0:POOL_HIST - 1] = hist_ref[1:POOL_HIST]
    pstate_ref[POOL_HIST - 1] = u


def _lane_group_reduce(v, op):
    sh = N_HEADS
    while sh < LANES:
        v = op(v, pltpu.roll(v, sh, axis=1))
        sh *= 2
    return v


def _diag_to_col(v):
    sub = lax.broadcasted_iota(jnp.int32, v.shape, 0)
    lane = lax.broadcasted_iota(jnp.int32, v.shape, 1)
    return jnp.sum(jnp.where(sub == lane, v, 0.0), axis=1, keepdims=True)


def _latent_rows(kbuf):
    n_pairs, _, page, width = kbuf.shape
    halves = []
    for half in range(2):
        cols = []
        for jj in range(KEYS_PER_ROW // 2):
            r0 = (half * (KEYS_PER_ROW // 2) + jj) * 8
            for par in range(2):
                cols.append(kbuf[:, par, r0:r0 + 8, :].reshape(n_pairs * 8, width))
        halves.append(jnp.concatenate(cols, axis=1))
    return jnp.concatenate(halves, axis=0).astype(BF16)


def _rotary_rows(rbuf):
    n_pairs = rbuf.shape[0]
    xt = jnp.swapaxes(rbuf[...], 1, 2)
    halves = []
    for half in range(2):
        cols = []
        for jj in range(KEYS_PER_ROW // 2):
            r0 = (half * (KEYS_PER_ROW // 2) + jj) * 8
            cols.append(xt[:, r0:r0 + 8, :].reshape(n_pairs * 8, LANES))
        halves.append(jnp.concatenate(cols, axis=1))
    return jnp.concatenate(halves, axis=0).astype(BF16)


def _attend_one(qat, qrt, knew, rnew, kbuf, rbuf):
    lane_a = lax.broadcasted_iota(jnp.int32, qat.shape, 1) // N_HEADS
    lane_r = lax.broadcasted_iota(jnp.int32, qrt.shape, 1) // N_HEADS
    zero = jnp.zeros((), BF16)
    wk = jnp.concatenate([jnp.where(lane_a == j, qat, zero) for j in range(KEYS_PER_ROW)], axis=0)
    wr = jnp.concatenate([jnp.where(lane_r == j, qrt, zero) for j in range(KEYS_PER_ROW)], axis=0)

    kb = _latent_rows(kbuf)
    rb = _rotary_rows(rbuf)
    hr = kb.shape[0] // 2
    s_lat = [_dot(kb[:hr], wk), _dot(kb[hr:], wk)]
    s_rot = [_dot(rb[:hr], wr), _dot(rb[hr:], wr)]
    s2 = jnp.concatenate([s_lat[0] + s_rot[0], s_lat[1] + s_rot[1]], axis=0)

    knew_b = knew.astype(BF16)
    rnew_b = rnew.astype(BF16)
    s_new = (_dot(jnp.broadcast_to(knew_b, (8, KV_RANK)), qat)
             + _dot(jnp.broadcast_to(rnew_b, (8, QK_ROPE)), qrt))

    m = jnp.max(s2, axis=0, keepdims=True)
    m = _lane_group_reduce(jnp.broadcast_to(m, (8, LANES)), jnp.maximum)
    m = jnp.maximum(m, s_new)
    p2 = jnp.exp2(s2 - m[0:1])
    p_new = jnp.exp2(s_new - m)
    l_lane = _lane_group_reduce(
        jnp.broadcast_to(jnp.sum(p2, axis=0, keepdims=True), (8, LANES)), jnp.add)
    p_new_col = _diag_to_col(p_new)
    l_col = _diag_to_col(l_lane) + p_new_col

    p2t = p2.T.astype(BF16)
    hs = KEYS_PER_ROW // 2
    hp, hk = hs * N_HEADS, hs * KV_RANK
    g_halves = [_dot(p2t[:hp], kb[:, :hk]), _dot(p2t[hp:], kb[:, hk:])]
    o = p_new_col.astype(BF16).astype(F32) * knew_b.astype(F32)
    for g in g_halves:
        for j in range(hs):
            o = o + g[j * N_HEADS:(j + 1) * N_HEADS, j * KV_RANK:(j + 1) * KV_RANK]
    return o / l_col


def _sample_attn_kernel(pt_ref, qat_ref, qrt_ref, knew_ref, rnew_ref, ck_hbm, krt_hbm, o_ref,
                        kbuf_a, rbuf_a, kbuf_b, rbuf_b, sems, *, n_pages):
    g = pl.program_id(0)
    ng = pl.num_programs(0)
    bufs = ((kbuf_a, rbuf_a), (kbuf_b, rbuf_b))

    def start_pair(seq, s, q):
        kbuf, rbuf = bufs[s]
        for par in range(2):
            pid = pt_ref[seq * n_pages + 2 * q + par]
            pltpu.make_async_copy(ck_hbm.at[pid], kbuf.at[q, par], sems.at[0, s]).start()
            pltpu.make_async_copy(krt_hbm.at[pid], rbuf.at[q, par * QK_ROPE:(par + 1) * QK_ROPE, :],
                                  sems.at[1, s]).start()

    def start_inline(seq, s):
        for q in range(n_pages // 2):
            start_pair(seq, s, q)

    def wait_all(s):
        kbuf, rbuf = bufs[s]
        pltpu.make_async_copy(kbuf, kbuf, sems.at[0, s]).wait()
        pltpu.make_async_copy(rbuf, rbuf, sems.at[1, s]).wait()

    @pl.when(g == 0)
    def _():
        def body(q, c):
            start_pair(0, 0, q)
            start_pair(1, 1, q)
            return c
        lax.fori_loop(0, n_pages // 2, body, 0)

    nxt = jnp.minimum(g + 1, ng - 1)
    wait_all(0)
    o_ref[0] = _attend_one(qat_ref[0], qrt_ref[0], knew_ref[0], rnew_ref[0], kbuf_a, rbuf_a)
    start_inline(2 * nxt, 0)
    wait_all(1)
    o_ref[1] = _attend_one(qat_ref[1], qrt_ref[1], knew_ref[1], rnew_ref[1], kbuf_b, rbuf_b)
    start_inline(2 * nxt + 1, 1)

    @pl.when(g == ng - 1)
    def _():
        wait_all(0)
        wait_all(1)


def _const_spec(shape):
    zeros = (0,) * len(shape)
    return pl.BlockSpec(shape, lambda *_: zeros)


def _prep_weights(g_pre, w_in, g_q, w_uq, g_kv, w_uk, w_uv, w_o_att, w_pool, pool_scale,
                  w_o_pool, w_out, g_final):
    d_model = w_in.shape[0]
    att_w = N_HEADS * V_DIM
    half = QK_ROPE // 2
    o_kv = Q_RANK
    o_kr = o_kv + KV_RANK
    o_za = o_kr + QK_ROPE
    o_up = o_za + att_w
    o_zp = o_up + POOL_WIDTH
    o_ga = o_zp + POOL_WIDTH
    o_gp = o_ga + d_model
    w_t = w_in.T
    w_kr = w_t[o_kr:o_za]
    w_a = jnp.concatenate([w_t[:o_kr], w_kr, -w_kr[half:], w_kr[:half]], axis=0)
    uq = w_uq.reshape(Q_RANK, N_HEADS, QK_NOPE + QK_ROPE)
    w_qn = uq[:, :, :QK_NOPE].reshape(Q_RANK, N_HEADS * QK_NOPE)
    x1, x2 = uq[:, :, QK_NOPE:QK_NOPE + half], uq[:, :, QK_NOPE + half:]
    w_qr = jnp.concatenate([x1, x2, -x2, x1], axis=2).reshape(Q_RANK, N_HEADS * LANES)
    bf = lambda a: a.astype(BF16)
    row = lambda a: a.reshape(1, -1).astype(F32)
    return dict(
        g_pre=row(g_pre), w_a=bf(w_a), w_za=bf(w_t[o_za:o_up]), w_up=bf(w_t[o_up:o_zp]),
        w_zp=bf(w_t[o_zp:o_ga]), w_ga=bf(w_t[o_ga:o_gp]), w_gp=bf(w_t[o_gp:]),
        g_q=row(g_q), w_qn=bf(w_qn), w_qr=bf(w_qr),
        w_ukt=bf(jnp.transpose(w_uk, (1, 2, 0))),
        g_kv=row(g_kv), w_pool=bf(w_pool), pscale=row(pool_scale), w_opool=bf(w_o_pool),
        w_uv=bf(jnp.transpose(w_uv, (1, 0, 2))),
        w_oatt=bf(w_o_att), w_out=bf(w_out), g_final=row(g_final))


_PROJ_WEIGHTS = ("g_pre", "w_a", "w_za", "w_up", "w_zp", "w_ga", "w_gp", "g_q", "w_qn", "w_qr",
                 "w_ukt", "g_kv", "w_pool", "pscale", "w_opool")


def _cos_sin(pos):
    half = QK_ROPE // 2
    inv = ROPE_BASE ** (-jnp.arange(half, dtype=F32) / half)
    ang = pos.astype(F32)[:, None] * inv[None, :]
    return jnp.cos(ang), jnp.sin(ang)


def _rope_table(pos):
    c, s = _cos_sin(pos)
    return jnp.concatenate([c, c, s, s], axis=1)


ROPE_BLOCK = 128


def _rope_factors(seq):
    ca, sa = _cos_sin(jnp.arange(0, seq, ROPE_BLOCK, dtype=jnp.int32))
    cb, sb = _cos_sin(jnp.arange(ROPE_BLOCK, dtype=jnp.int32))
    rope_a = jnp.stack([jnp.concatenate([ca, ca, sa, sa], axis=1),
                        jnp.concatenate([-sa, -sa, ca, ca], axis=1)], axis=1)
    rope_b = jnp.stack([jnp.concatenate([cb] * 4, axis=1),
                        jnp.concatenate([sb] * 4, axis=1)], axis=0)
    return rope_a, rope_b


def _prompt_proj(x, rope_a, rope_b, hist, wts, tm, t_attn):
    seq, d_model = x.shape
    n = seq // tm
    per_attn = t_attn // tm
    blocks = tm // ROPE_BLOCK
    hist_rows = hist.shape[0]
    ws = [wts[k] for k in _PROJ_WEIGHTS]
    rows = lambda width: pl.BlockSpec((tm, width), lambda i: (i, 0))
    out_shape = (
        jax.ShapeDtypeStruct((N_HEADS, seq, QCAT), BF16),
        jax.ShapeDtypeStruct((seq, QCAT), BF16),
        jax.ShapeDtypeStruct((seq // t_attn, KV_RANK, t_attn), BF16),
        jax.ShapeDtypeStruct((seq, KV_RANK), F32),
        jax.ShapeDtypeStruct((QK_ROPE, seq), F32),
        jax.ShapeDtypeStruct((seq, d_model), BF16),
        jax.ShapeDtypeStruct((seq, d_model), BF16),
        jax.ShapeDtypeStruct((seq, d_model), BF16),
        jax.ShapeDtypeStruct((hist_rows, POOL_WIDTH), F32),
        jax.ShapeDtypeStruct((n, 8, LANES), F32),
    )
    out_specs = (
        pl.BlockSpec((N_HEADS, tm, QCAT), lambda i: (0, i, 0)),
        rows(QCAT),
        pl.BlockSpec((1, KV_RANK, tm), lambda i: (i // per_attn, 0, i % per_attn)),
        rows(KV_RANK), pl.BlockSpec((QK_ROPE, tm), lambda i: (0, i)),
        rows(d_model), rows(d_model), rows(d_model),
        _const_spec((hist_rows, POOL_WIDTH)),
        pl.BlockSpec((1, 8, LANES), lambda i: (i, 0, 0)),
    )
    return pl.pallas_call(
        functools.partial(_prompt_proj_kernel, tm=tm),
        grid=(n,),
        in_specs=[rows(d_model), pl.BlockSpec((blocks, 2, LANES), lambda i: (i, 0, 0)),
                  _const_spec(rope_b.shape), _const_spec(hist.shape)]
        + [_const_spec(w.shape) for w in ws],
        out_specs=out_specs,
        out_shape=out_shape,
        scratch_shapes=[pltpu.VMEM((hist_rows + tm, POOL_WIDTH), F32),
                        pltpu.VMEM((8, LANES), F32)],
        compiler_params=pltpu.CompilerParams(
            dimension_semantics=("arbitrary",), vmem_limit_bytes=VMEM_LIMIT),
    )(x, rope_a, rope_b, hist, *ws)


def _prompt_attn(safe, qcat, kcat, vt, t):
    seq = kcat.shape[0]
    n = seq // t
    grid_spec = pltpu.PrefetchScalarGridSpec(
        num_scalar_prefetch=1,
        grid=(n,),
        in_specs=[pl.BlockSpec((N_HEADS, t, QCAT), lambda i, s: (0, i, 0)),
                  pl.BlockSpec(memory_space=pltpu.VMEM),
                  pl.BlockSpec(memory_space=pltpu.VMEM)],
        out_specs=pl.BlockSpec((t, N_HEADS * KV_RANK), lambda i, s: (i, 0)),
        scratch_shapes=[pltpu.VMEM((N_HEADS, 1, t), F32),
                        pltpu.VMEM((N_HEADS, 1, t), F32),
                        pltpu.VMEM((N_HEADS, KV_RANK, t), F32)],
    )
    return pl.pallas_call(
        functools.partial(_prompt_attn_kernel, t=t),
        grid_spec=grid_spec,
        out_shape=jax.ShapeDtypeStruct((seq, N_HEADS * KV_RANK), BF16),
        compiler_params=pltpu.CompilerParams(
            dimension_semantics=("arbitrary",), vmem_limit_bytes=VMEM_LIMIT),
    )(safe, qcat, kcat, vt)


def _epilogue(o, sz, ga, yp, x, wts, tm):
    n_rows, d_model = x.shape
    rows = pl.BlockSpec((tm, d_model), lambda i: (i, 0))
    ws = [wts[k] for k in ("w_uv", "w_oatt", "w_out", "g_final")]
    return pl.pallas_call(
        _epilogue_kernel,
        grid=(n_rows // tm,),
        in_specs=[rows] * 5 + [_const_spec(w.shape) for w in ws],
        out_specs=rows,
        out_shape=jax.ShapeDtypeStruct((n_rows, d_model), F32),
        compiler_params=pltpu.CompilerParams(
            dimension_semantics=("arbitrary",), vmem_limit_bytes=VMEM_LIMIT),
    )(o, sz, ga, yp, x, *ws)


def _sample_proj(x, cs, hist, wts, past_len):
    b, d_model = x.shape
    ws = [wts[k] for k in _PROJ_WEIGHTS]
    wide = jax.ShapeDtypeStruct((b, d_model), BF16)
    out_shape = (
        jax.ShapeDtypeStruct((b, N_HEADS * LANES), F32),
        jax.ShapeDtypeStruct((b, N_HEADS * LANES), F32),
        jax.ShapeDtypeStruct((b, KV_RANK), F32),
        jax.ShapeDtypeStruct((b, QK_ROPE), F32),
        wide, wide, wide,
        jax.ShapeDtypeStruct(hist.shape, F32),
    )
    args = (x, cs, hist, *ws)
    return pl.pallas_call(
        functools.partial(_sample_proj_kernel, past_len=past_len),
        grid=(1,),
        in_specs=[_const_spec(a.shape) for a in args],
        out_specs=tuple(_const_spec(s.shape) for s in out_shape),
        out_shape=out_shape,
        compiler_params=pltpu.CompilerParams(
            dimension_semantics=("arbitrary",), vmem_limit_bytes=VMEM_LIMIT),
    )(*args)


def _sample_attn(page_table, qat, qrt, knew, rnew, cache_ckv, cache_krope):
    b, n_pages = page_table.shape
    page = cache_ckv.shape[1]
    assert b % 2 == 0 and n_pages % 2 == 0
    pair = lambda rows, width: pl.BlockSpec((2, rows, width), lambda i, pt: (i, 0, 0))
    kbuf = pltpu.VMEM((n_pages // 2, 2, page, KV_RANK), F32)
    rbuf = pltpu.VMEM((n_pages // 2, 2 * QK_ROPE, page), F32)
    grid_spec = pltpu.PrefetchScalarGridSpec(
        num_scalar_prefetch=1,
        grid=(b // 2,),
        in_specs=[pair(KV_RANK, LANES), pair(QK_ROPE, LANES), pair(1, KV_RANK), pair(1, QK_ROPE),
                  pl.BlockSpec(memory_space=pl.ANY),
                  pl.BlockSpec(memory_space=pl.ANY)],
        out_specs=pair(N_HEADS, KV_RANK),
        scratch_shapes=[kbuf, rbuf, kbuf, rbuf, pltpu.SemaphoreType.DMA((2, 2))],
    )
    return pl.pallas_call(
        functools.partial(_sample_attn_kernel, n_pages=n_pages),
        grid_spec=grid_spec,
        out_shape=jax.ShapeDtypeStruct((b, N_HEADS, KV_RANK), F32),
        compiler_params=pltpu.CompilerParams(
            dimension_semantics=("arbitrary",), vmem_limit_bytes=VMEM_LIMIT),
    )(page_table.reshape(-1), qat, qrt, knew, rnew, cache_ckv, jnp.swapaxes(cache_krope, 1, 2))


def kernel(x_prompt, x_sample, cache_ckv, cache_krope, state_pool, page_table, g_pre, w_in, g_q,
           w_uq, g_kv, w_uk, w_uv, w_o_att, w_pool, pool_scale, w_o_pool, w_out, g_final):
    depth = g_pre.shape[0]
    batch, seq, d_model = x_prompt.shape
    dec_b, dec_seq, _ = x_sample.shape
    assert depth == 1 and batch == 1 and dec_seq == 1
    past_len = page_table.shape[1] * cache_ckv.shape[2]
    wts = _prep_weights(g_pre[0], w_in[0], g_q[0], w_uq[0], g_kv[0], w_uk[0], w_uv[0], w_o_att[0],
                        w_pool[0], pool_scale[0], w_o_pool[0], w_out[0], g_final)

    tile = 512
    proj_tile = 512
    hist_rows = 16
    xp = x_prompt.reshape(seq, d_model)
    rope_a, rope_b = _rope_factors(seq)
    hist0 = jnp.zeros((hist_rows, POOL_WIDTH), F32)
    qcat, kcat, vt, ckv_p, krt_p, sz, ga, yp, pstate, slack = _prompt_proj(
        xp, rope_a, rope_b, hist0, wts, proj_tile, tile)
    slack = jnp.max(slack[:, 0, 0].reshape(seq // tile, tile // proj_tile), axis=1)
    safe = (slack <= SAFE_SLACK).astype(jnp.int32)
    o_p = _prompt_attn(safe, qcat, kcat, vt, tile)
    y_p = _epilogue(o_p, sz, ga, yp, xp, wts, tile)

    xs = x_sample.reshape(dec_b, d_model)
    cs_s = _rope_table(jnp.full((dec_b,), past_len, jnp.int32))
    hist_s = jnp.transpose(state_pool[0], (1, 0, 2))
    qa, qr, ckv_s, kr_s, sz_s, ga_s, yp_s, pstate_s = _sample_proj(xs, cs_s, hist_s, wts, past_len)
    qa3 = qa.reshape(dec_b, N_HEADS, LANES)
    qr3 = qr.reshape(dec_b, N_HEADS, LANES)[:, :, :QK_ROPE]
    qat = jnp.tile(jnp.transpose(qa3, (0, 2, 1)), (1, 1, KEYS_PER_ROW)).astype(BF16)
    qrt = jnp.tile(jnp.transpose(qr3, (0, 2, 1)), (1, 1, KEYS_PER_ROW)).astype(BF16)
    o_s = _sample_attn(page_table, qat, qrt, ckv_s.reshape(dec_b, 1, KV_RANK),
                       kr_s.reshape(dec_b, 1, QK_ROPE), cache_ckv[0], cache_krope[0])
    o_s = o_s.reshape(dec_b, N_HEADS * KV_RANK).astype(BF16)
    y_s = _epilogue(o_s, sz_s, ga_s, yp_s, xs, wts, dec_b)

    return (y_p.reshape(batch, seq, d_model),
            y_s.reshape(dec_b, 1, d_model),
            ckv_p.reshape(1, batch, seq, KV_RANK),
            krt_p.T.reshape(1, batch, seq, QK_ROPE),
            pstate[hist_rows - POOL_HIST:].reshape(1, batch, POOL_HIST, POOL_WIDTH),
            ckv_s.reshape(1, dec_b, 1, KV_RANK),
            kr_s.reshape(1, dec_b, 1, QK_ROPE),
            jnp.transpose(pstate_s, (1, 0, 2)).reshape(1, dec_b, POOL_HIST, POOL_WIDTH))
```

```python
import functools

import jax
import jax.numpy as jnp
from jax import lax
from jax.experimental import pallas as pl
from jax.experimental.pallas import tpu as pltpu

F32 = jnp.float32
BF16 = jnp.bfloat16

N_HEADS = 8
QK_NOPE = 128
QK_ROPE = 64
V_DIM = 128
Q_RANK = 256
KV_RANK = 128
ROPE_BASE = 10000.0
POOL_WINDOWS = (2, 4, 8, 16)
POOL_GROUP_DIM = 128
POOL_WIDTH = len(POOL_WINDOWS) * POOL_GROUP_DIM
POOL_HIST = 15
EPS = 1e-6
NEG_INF = -1e30
SOFTMAX_SCALE = (QK_NOPE + QK_ROPE) ** -0.5
LOG2_E = 1.4426950408889634
Q_SCALE = SOFTMAX_SCALE * LOG2_E
SAFE_SLACK = 64.0

LANES = 128
KEYS_PER_ROW = LANES // N_HEADS
DECODE_SLOTS = 3
QCAT = 2 * LANES
VMEM_LIMIT = 56 * 1024 * 1024


def _dot(a, b):
    return jnp.dot(a, b, preferred_element_type=F32)


def _dot_nt(a, b):
    return lax.dot_general(a, b, (((1,), (1,)), ((), ())), preferred_element_type=F32)


def _rms(x, g):
    return x * lax.rsqrt(jnp.mean(x * x, axis=-1, keepdims=True) + EPS) * g


def _silu(x):
    return x * jax.nn.sigmoid(x)


def _rope_pair(t):
    return t + pltpu.roll(t, QK_ROPE, axis=1)


def _queries_keys(za, cs, g_q, w_qn, w_qr, w_ukt, g_kv):
    qn = _rms(za[:, :Q_RANK], g_q).astype(BF16)
    ckv = _rms(za[:, Q_RANK:Q_RANK + KV_RANK], g_kv)
    kr2 = _rope_pair(za[:, Q_RANK + KV_RANK:] * cs)
    q_nope = _dot(qn, w_qn)
    q_rope = _dot(qn, w_qr)
    qa, qr = [], []
    for hd in range(N_HEADS):
        sl = slice(hd * LANES, (hd + 1) * LANES)
        qa.append(_dot(q_nope[:, sl].astype(BF16), w_ukt[hd]) * Q_SCALE)
        qr.append(_rope_pair(q_rope[:, sl] * cs) * Q_SCALE)
    return qa, qr, ckv, kr2


def _pool_branch(h, win_sum, u, rcnt, w_zp, w_gp, w_pool, pool_scale, w_o_pool):
    mixed = []
    for g in range(len(POOL_WINDOWS)):
        sl = slice(g * POOL_GROUP_DIM, (g + 1) * POOL_GROUP_DIM)
        pooled = win_sum[g] * rcnt[g] - u[:, sl]
        mixed.append(_dot(pooled.astype(BF16), w_pool[g]))
    p = jnp.concatenate(mixed, axis=1) * pool_scale
    z_pool = _dot_nt(h, w_zp)
    y_pool = _dot((p * _silu(z_pool)).astype(BF16), w_o_pool)
    return jax.nn.sigmoid(_dot_nt(h, w_gp)) * y_pool


def _gates_and_pool_input(h, w_up_ref, w_za_ref, w_ga_ref, sz_ref, ga_ref):
    sz_ref[...] = _silu(_dot_nt(h, w_za_ref[...])).astype(sz_ref.dtype)
    ga_ref[...] = jax.nn.sigmoid(_dot_nt(h, w_ga_ref[...])).astype(ga_ref.dtype)
    return _dot_nt(h, w_up_ref[...])


def _prompt_proj_kernel(x_ref, rope_a_ref, rope_b_ref, hist_ref, g_pre_ref, w_a_ref, w_za_ref, w_up_ref,
                        w_zp_ref, w_ga_ref, w_gp_ref, g_q_ref, w_qn_ref, w_qr_ref, w_ukt_ref,
                        g_kv_ref, w_pool_ref, pscale_ref, w_opool_ref,
                        qcat_ref, kcat_ref, vt_ref, ckv_ref, kr_ref, sz_ref, ga_ref, yp_ref,
                        pstate_ref, slack_ref, uext_ref, kmax_ref, *, tm):
    i = pl.program_id(0)
    hist_rows = uext_ref.shape[0] - tm

    @pl.when(i == 0)
    def _():
        uext_ref[0:hist_rows, :] = hist_ref[...]
        kmax_ref[...] = jnp.zeros(kmax_ref.shape, F32)

    cs = jnp.concatenate(
        [rope_a_ref[a, 0:1, :] * rope_b_ref[0] + rope_a_ref[a, 1:2, :] * rope_b_ref[1]
         for a in range(rope_a_ref.shape[0])], axis=0)
    h = _rms(x_ref[...], g_pre_ref[...]).astype(BF16)
    za = _dot_nt(h, w_a_ref[...])
    qa, qr, ckv, kr2 = _queries_keys(za, cs, g_q_ref[...], w_qn_ref[...], w_qr_ref[...], w_ukt_ref,
                                     g_kv_ref[...])
    lane = lax.broadcasted_iota(jnp.int32, kr2.shape, 1)
    is_rope = lane < QK_ROPE
    is_shift = lane == QK_ROPE
    rsum = lambda a: jnp.sum(a, axis=1, keepdims=True)
    ckv_b = ckv.astype(BF16).astype(F32)
    kr_b = jnp.where(is_rope, kr2, 0.0).astype(BF16).astype(F32)
    k_norm2 = jnp.max(rsum(ckv_b * ckv_b + kr_b * kr_b), axis=0, keepdims=True)
    k_norm2 = jnp.maximum(k_norm2, kmax_ref[0:1, 0:1])
    kmax_ref[...] = jnp.broadcast_to(k_norm2, kmax_ref.shape)
    slack = None
    for hd in range(N_HEADS):
        qa_b = qa[hd].astype(BF16)
        qr_b = jnp.where(is_rope, qr[hd], 0.0).astype(BF16).astype(F32)
        qa_f = qa_b.astype(F32)
        c = jnp.sqrt(rsum(qa_f * qa_f + qr_b * qr_b) * k_norm2) + 1.0
        c = c.astype(BF16).astype(F32)
        own = rsum(qa_f * ckv_b + qr_b * kr_b)
        worst = jnp.max(c - own, axis=0, keepdims=True)
        slack = worst if slack is None else jnp.maximum(slack, worst)
        qcat_ref[hd, :, 0:LANES] = qa_b
        qcat_ref[hd, :, LANES:QCAT] = jnp.where(is_shift, -c, qr_b).astype(BF16)
    slack_ref[0] = jnp.broadcast_to(slack, slack_ref.shape[1:])
    kcat_ref[:, 0:LANES] = ckv.astype(BF16)
    kcat_ref[:, LANES:QCAT] = jnp.where(is_shift, 1.0, kr_b).astype(BF16)
    vt_ref[0] = ckv.T.astype(BF16)
    ckv_ref[...] = ckv
    kr_ref[...] = kr2.T[:QK_ROPE, :]

    u = _gates_and_pool_input(h, w_up_ref, w_za_ref, w_ga_ref, sz_ref, ga_ref)
    uext_ref[hist_rows:, :] = u
    pos = i * tm + lax.broadcasted_iota(jnp.int32, (tm, 1), 0)
    win_sum, rcnt = [], []
    for g, w in enumerate(POOL_WINDOWS):
        c0 = g * POOL_GROUP_DIM
        acc = u[:, c0:c0 + POOL_GROUP_DIM]
        for s in range(1, w):
            acc = acc + uext_ref[hist_rows - s:hist_rows - s + tm, c0:c0 + POOL_GROUP_DIM]
        win_sum.append(acc)
        rcnt.append(1.0 / jnp.minimum(w, pos + 1).astype(F32))
    yp_ref[...] = _pool_branch(h, win_sum, u, rcnt, w_zp_ref[...], w_gp_ref[...], w_pool_ref,
                               pscale_ref[...], w_opool_ref[...]).astype(yp_ref.dtype)
    tail = uext_ref[tm:tm + hist_rows, :]
    pstate_ref[...] = tail
    uext_ref[0:hist_rows, :] = tail


def _prompt_attn_kernel(safe_ref, q_ref, k_ref, vt_ref, o_ref, m_sc, l_sc, acc_sc, *, t):
    i = pl.program_id(0)
    l_sc[...] = jnp.zeros(l_sc.shape, F32)
    acc_sc[...] = jnp.zeros(acc_sc.shape, F32)

    def chunks(specs, shifted):
        kcs = [k_ref[pl.ds(pl.multiple_of(j * t, t), t), :] for j, _ in specs]
        vts = [vt_ref[j] for j, _ in specs]
        units = [(c, hd) for c in range(len(specs)) for hd in range(N_HEADS)]
        if any(masked for _, masked in specs):
            key = lax.broadcasted_iota(jnp.int32, (t, t), 0)
            qry = lax.broadcasted_iota(jnp.int32, (t, t), 1)
            keep = key <= qry
        scores = [_dot_nt(kcs[c], q_ref[hd]) for c, hd in units]
        for (c, hd), st in zip(units, scores):
            if specs[c][1]:
                st = jnp.where(keep, st, NEG_INF)
            if shifted:
                p = jnp.exp2(st)
                l_sc[hd] += jnp.sum(p, axis=0, keepdims=True)
                acc_sc[hd] += _dot(vts[c], p.astype(BF16))
            else:
                m_prev = m_sc[hd]
                m_new = jnp.maximum(m_prev, jnp.max(st, axis=0, keepdims=True))
                alpha = jnp.exp2(m_prev - m_new)
                p = jnp.exp2(st - m_new)
                l_sc[hd] = alpha * l_sc[hd] + jnp.sum(p, axis=0, keepdims=True)
                acc_sc[hd] = alpha * acc_sc[hd] + _dot(vts[c], p.astype(BF16))
                m_sc[hd] = m_new

    def sweep(shifted, width):
        def body(jw, carry):
            chunks([(jw * width + c, False) for c in range(width)], shifted)
            return carry
        lax.fori_loop(0, i // width, body, 0)
        for rem in range(width):
            @pl.when(i % width == rem)
            def _(rem=rem):
                chunks([(i - rem + c, False) for c in range(rem)] + [(i, True)], shifted)

    safe = safe_ref[i] != 0

    @pl.when(safe)
    def _():
        sweep(True, 3)

    @pl.when(jnp.logical_not(safe))
    def _():
        m_sc[...] = jnp.full(m_sc.shape, NEG_INF, F32)
        sweep(False, 1)

    for hd in range(N_HEADS):
        o = acc_sc[hd] / l_sc[hd]
        o_ref[:, hd * KV_RANK:(hd + 1) * KV_RANK] = o.T.astype(BF16)


def _epilogue_kernel(o_ref, sz_ref, ga_ref, yp_ref, x_ref, w_uv_ref, w_oatt_ref, w_out_ref,
                     g_final_ref, y_ref):
    o = o_ref[...]
    o2 = jnp.concatenate(
        [_dot(o[:, hd * KV_RANK:(hd + 1) * KV_RANK], w_uv_ref[hd]) for hd in range(N_HEADS)], axis=1)
    y_att = _dot((o2 * sz_ref[...].astype(F32)).astype(BF16), w_oatt_ref[...])
    merged = ga_ref[...].astype(F32) * y_att + yp_ref[...].astype(F32)
    xo = x_ref[...] + _dot(merged.astype(BF16), w_out_ref[...])
    y_ref[...] = _rms(xo, g_final_ref[...])


def _sample_proj_kernel(x_ref, cs_ref, hist_ref, g_pre_ref, w_a_ref, w_za_ref, w_up_ref,
                        w_zp_ref, w_ga_ref, w_gp_ref, g_q_ref, w_qn_ref, w_qr_ref, w_ukt_ref,
                        g_kv_ref, w_pool_ref, pscale_ref, w_opool_ref,
                        qa_ref, qr_ref, ckv_ref, kr_ref, sz_ref, ga_ref, yp_ref, pstate_ref,
                        *, past_len):
    h = _rms(x_ref[...], g_pre_ref[...]).astype(BF16)
    za = _dot_nt(h, w_a_ref[...])
    qa, qr, ckv, kr2 = _queries_keys(za, cs_ref[...], g_q_ref[...], w_qn_ref[...], w_qr_ref[...],
                                     w_ukt_ref, g_kv_ref[...])
    for hd in range(N_HEADS):
        qa_ref[:, hd * LANES:(hd + 1) * LANES] = qa[hd]
        qr_ref[:, hd * LANES:(hd + 1) * LANES] = qr[hd]
    ckv_ref[...] = ckv
    kr_ref[...] = kr2[:, :QK_ROPE]

    u = _gates_and_pool_input(h, w_up_ref, w_za_ref, w_ga_ref, sz_ref, ga_ref)
    win_sum, rcnt = [], []
    for g, w in enumerate(POOL_WINDOWS):
        c0 = g * POOL_GROUP_DIM
        acc = u[:, c0:c0 + POOL_GROUP_DIM]
        for s in range(1, w):
            acc = acc + hist_ref[POOL_HIST - s, :, c0:c0 + POOL_GROUP_DIM]
        win_sum.append(acc)
        rcnt.append(1.0 / float(min(w, past_len + 1)))
    yp_ref[...] = _pool_branch(h, win_sum, u, rcnt, w_zp_ref[...], w_gp_ref[...], w_pool_ref,
                               pscale_ref[...], w_opool_ref[...]).astype(yp_ref.dtype)
    pstate_ref[0:POOL_HIST - 1] = hist_ref[1:POOL_HIST]
    pstate_ref[POOL_HIST - 1] = u


def _lane_group_reduce(v, op):
    sh = N_HEADS
    while sh < LANES:
        v = op(v, pltpu.roll(v, sh, axis=1))
        sh *= 2
    return v


def _diag_to_col(v):
    sub = lax.broadcasted_iota(jnp.int32, v.shape, 0)
    lane = lax.broadcasted_iota(jnp.int32, v.shape, 1)
    return jnp.sum(jnp.where(sub == lane, v, 0.0), axis=1, keepdims=True)


def _latent_rows(kbuf):
    n_pairs, _, page, width = kbuf.shape
    halves = []
    for half in range(2):
        cols = []
        for jj in range(KEYS_PER_ROW // 2):
            r0 = (half * (KEYS_PER_ROW // 2) + jj) * 8
            for par in range(2):
                cols.append(kbuf[:, par, r0:r0 + 8, :].reshape(n_pairs * 8, width))
        halves.append(jnp.concatenate(cols, axis=1))
    return jnp.concatenate(halves, axis=0).astype(BF16)


def _rotary_rows(rbuf):
    n_pairs = rbuf.shape[0]
    xt = jnp.swapaxes(rbuf[...], 1, 2)
    halves = []
    for half in range(2):
        cols = []
        for jj in range(KEYS_PER_ROW // 2):
            r0 = (half * (KEYS_PER_ROW // 2) + jj) * 8
            cols.append(xt[:, r0:r0 + 8, :].reshape(n_pairs * 8, LANES))
        halves.append(jnp.concatenate(cols, axis=1))
    return jnp.concatenate(halves, axis=0).astype(BF16)


def _attend_one(qat, qrt, knew, rnew, kbuf, rbuf):
    lane_a = lax.broadcasted_iota(jnp.int32, qat.shape, 1) // N_HEADS
    lane_r = lax.broadcasted_iota(jnp.int32, qrt.shape, 1) // N_HEADS
    zero = jnp.zeros((), BF16)
    wk = jnp.concatenate([jnp.where(lane_a == j, qat, zero) for j in range(KEYS_PER_ROW)], axis=0)
    wr = jnp.concatenate([jnp.where(lane_r == j, qrt, zero) for j in range(KEYS_PER_ROW)], axis=0)

    kb = _latent_rows(kbuf)
    rb = _rotary_rows(rbuf)
    hr = kb.shape[0] // 2
    s_lat = [_dot(kb[:hr], wk), _dot(kb[hr:], wk)]
    s_rot = [_dot(rb[:hr], wr), _dot(rb[hr:], wr)]
    s2 = jnp.concatenate([s_lat[0] + s_rot[0], s_lat[1] + s_rot[1]], axis=0)

    knew_b = knew.astype(BF16)
    rnew_b = rnew.astype(BF16)
    s_new = (_dot(jnp.broadcast_to(knew_b, (8, KV_RANK)), qat)
             + _dot(jnp.broadcast_to(rnew_b, (8, QK_ROPE)), qrt))

    m = jnp.max(s2, axis=0, keepdims=True)
    m = _lane_group_reduce(jnp.broadcast_to(m, (8, LANES)), jnp.maximum)
    m = jnp.maximum(m, s_new)
    p2 = jnp.exp2(s2 - m[0:1])
    p_new = jnp.exp2(s_new - m)
    l_lane = _lane_group_reduce(
        jnp.broadcast_to(jnp.sum(p2, axis=0, keepdims=True), (8, LANES)), jnp.add)
    p_new_col = _diag_to_col(p_new)
    l_col = _diag_to_col(l_lane) + p_new_col

    p2t = p2.T.astype(BF16)
    hs = KEYS_PER_ROW // 2
    hp, hk = hs * N_HEADS, hs * KV_RANK
    g_halves = [_dot(p2t[:hp], kb[:, :hk]), _dot(p2t[hp:], kb[:, hk:])]
    o = p_new_col.astype(BF16).astype(F32) * knew_b.astype(F32)
    for g in g_halves:
        for j in range(hs):
            o = o + g[j * N_HEADS:(j + 1) * N_HEADS, j * KV_RANK:(j + 1) * KV_RANK]
    return o / l_col


def _sample_attn_kernel(pt_ref, qat_ref, qrt_ref, knew_ref, rnew_ref, ck_hbm, krt_hbm, o_ref,
                        kbuf, rbuf, sems, *, n_pages, per_step):
    g = pl.program_id(0)
    n_seq = pl.num_programs(0) * per_step

    def start_pair(seq, slot, q):
        for par in range(2):
            pid = pt_ref[seq * n_pages + 2 * q + par]
            pltpu.make_async_copy(ck_hbm.at[pid], kbuf.at[slot, q, par], sems.at[0, slot]).start()
            pltpu.make_async_copy(
                krt_hbm.at[pid], rbuf.at[slot, q, par * QK_ROPE:(par + 1) * QK_ROPE, :],
                sems.at[1, slot]).start()

    def wait_all(slot):
        pltpu.make_async_copy(kbuf.at[slot], kbuf.at[slot], sems.at[0, slot]).wait()
        pltpu.make_async_copy(rbuf.at[slot], rbuf.at[slot], sems.at[1, slot]).wait()

    @pl.when(g == 0)
    def _():
        def body(q, c):
            for slot in range(DECODE_SLOTS):
                start_pair(slot, slot, q)
            return c
        lax.fori_loop(0, n_pages // 2, body, 0)

    for k in range(per_step):
        seq = g * per_step + k
        slot = seq % DECODE_SLOTS
        wait_all(slot)
        o_ref[k] = _attend_one(qat_ref[k], qrt_ref[k], knew_ref[k], rnew_ref[k],
                               kbuf.at[slot], rbuf.at[slot])
        refill = jnp.minimum(seq + DECODE_SLOTS, n_seq - 1)
        for q in range(n_pages // 2):
            start_pair(refill, slot, q)

    @pl.when(g == pl.num_programs(0) - 1)
    def _():
        for slot in range(DECODE_SLOTS):
            wait_all(slot)


def _const_spec(shape):
    zeros = (0,) * len(shape)
    return pl.BlockSpec(shape, lambda *_: zeros)


def _prep_weights(g_pre, w_in, g_q, w_uq, g_kv, w_uk, w_uv, w_o_att, w_pool, pool_scale,
                  w_o_pool, w_out, g_final):
    d_model = w_in.shape[0]
    att_w = N_HEADS * V_DIM
    half = QK_ROPE // 2
    o_kv = Q_RANK
    o_kr = o_kv + KV_RANK
    o_za = o_kr + QK_ROPE
    o_up = o_za + att_w
    o_zp = o_up + POOL_WIDTH
    o_ga = o_zp + POOL_WIDTH
    o_gp = o_ga + d_model
    w_t = w_in.T
    w_kr = w_t[o_kr:o_za]
    w_a = jnp.concatenate([w_t[:o_kr], w_kr, -w_kr[half:], w_kr[:half]], axis=0)
    uq = w_uq.reshape(Q_RANK, N_HEADS, QK_NOPE + QK_ROPE)
    w_qn = uq[:, :, :QK_NOPE].reshape(Q_RANK, N_HEADS * QK_NOPE)
    x1, x2 = uq[:, :, QK_NOPE:QK_NOPE + half], uq[:, :, QK_NOPE + half:]
    w_qr = jnp.concatenate([x1, x2, -x2, x1], axis=2).reshape(Q_RANK, N_HEADS * LANES)
    bf = lambda a: a.astype(BF16)
    row = lambda a: a.reshape(1, -1).astype(F32)
    return dict(
        g_pre=row(g_pre), w_a=bf(w_a), w_za=bf(w_t[o_za:o_up]), w_up=bf(w_t[o_up:o_zp]),
        w_zp=bf(w_t[o_zp:o_ga]), w_ga=bf(w_t[o_ga:o_gp]), w_gp=bf(w_t[o_gp:]),
        g_q=row(g_q), w_qn=bf(w_qn), w_qr=bf(w_qr),
        w_ukt=bf(jnp.transpose(w_uk, (1, 2, 0))),
        g_kv=row(g_kv), w_pool=bf(w_pool), pscale=row(pool_scale), w_opool=bf(w_o_pool),
        w_uv=bf(jnp.transpose(w_uv, (1, 0, 2))),
        w_oatt=bf(w_o_att), w_out=bf(w_out), g_final=row(g_final))


_PROJ_WEIGHTS = ("g_pre", "w_a", "w_za", "w_up", "w_zp", "w_ga", "w_gp", "g_q", "w_qn", "w_qr",
                 "w_ukt", "g_kv", "w_pool", "pscale", "w_opool")


def _cos_sin(pos):
    half = QK_ROPE // 2
    inv = ROPE_BASE ** (-jnp.arange(half, dtype=F32) / half)
    ang = pos.astype(F32)[:, None] * inv[None, :]
    return jnp.cos(ang), jnp.sin(ang)


def _rope_table(pos):
    c, s = _cos_sin(pos)
    return jnp.concatenate([c, c, s, s], axis=1)


ROPE_BLOCK = 128


def _rope_factors(seq):
    ca, sa = _cos_sin(jnp.arange(0, seq, ROPE_BLOCK, dtype=jnp.int32))
    cb, sb = _cos_sin(jnp.arange(ROPE_BLOCK, dtype=jnp.int32))
    rope_a = jnp.stack([jnp.concatenate([ca, ca, sa, sa], axis=1),
                        jnp.concatenate([-sa, -sa, ca, ca], axis=1)], axis=1)
    rope_b = jnp.stack([jnp.concatenate([cb] * 4, axis=1),
                        jnp.concatenate([sb] * 4, axis=1)], axis=0)
    return rope_a, rope_b


def _prompt_proj(x, rope_a, rope_b, hist, wts, tm, t_attn):
    seq, d_model = x.shape
    n = seq // tm
    per_attn = t_attn // tm
    blocks = tm // ROPE_BLOCK
    hist_rows = hist.shape[0]
    ws = [wts[k] for k in _PROJ_WEIGHTS]
    rows = lambda width: pl.BlockSpec((tm, width), lambda i: (i, 0))
    out_shape = (
        jax.ShapeDtypeStruct((N_HEADS, seq, QCAT), BF16),
        jax.ShapeDtypeStruct((seq, QCAT), BF16),
        jax.ShapeDtypeStruct((seq // t_attn, KV_RANK, t_attn), BF16),
        jax.ShapeDtypeStruct((seq, KV_RANK), F32),
        jax.ShapeDtypeStruct((QK_ROPE, seq), F32),
        jax.ShapeDtypeStruct((seq, d_model), BF16),
        jax.ShapeDtypeStruct((seq, d_model), BF16),
        jax.ShapeDtypeStruct((seq, d_model), BF16),
        jax.ShapeDtypeStruct((hist_rows, POOL_WIDTH), F32),
        jax.ShapeDtypeStruct((n, 8, LANES), F32),
    )
    out_specs = (
        pl.BlockSpec((N_HEADS, tm, QCAT), lambda i: (0, i, 0)),
        rows(QCAT),
        pl.BlockSpec((1, KV_RANK, tm), lambda i: (i // per_attn, 0, i % per_attn)),
        rows(KV_RANK), pl.BlockSpec((QK_ROPE, tm), lambda i: (0, i)),
        rows(d_model), rows(d_model), rows(d_model),
        _const_spec((hist_rows, POOL_WIDTH)),
        pl.BlockSpec((1, 8, LANES), lambda i: (i, 0, 0)),
    )
    return pl.pallas_call(
        functools.partial(_prompt_proj_kernel, tm=tm),
        grid=(n,),
        in_specs=[rows(d_model), pl.BlockSpec((blocks, 2, LANES), lambda i: (i, 0, 0)),
                  _const_spec(rope_b.shape), _const_spec(hist.shape)]
        + [_const_spec(w.shape) for w in ws],
        out_specs=out_specs,
        out_shape=out_shape,
        scratch_shapes=[pltpu.VMEM((hist_rows + tm, POOL_WIDTH), F32),
                        pltpu.VMEM((8, LANES), F32)],
        compiler_params=pltpu.CompilerParams(
            dimension_semantics=("arbitrary",), vmem_limit_bytes=VMEM_LIMIT),
    )(x, rope_a, rope_b, hist, *ws)


def _prompt_attn(safe, qcat, kcat, vt, t):
    seq = kcat.shape[0]
    n = seq // t
    grid_spec = pltpu.PrefetchScalarGridSpec(
        num_scalar_prefetch=1,
        grid=(n,),
        in_specs=[pl.BlockSpec((N_HEADS, t, QCAT), lambda i, s: (0, i, 0)),
                  pl.BlockSpec(memory_space=pltpu.VMEM),
                  pl.BlockSpec(memory_space=pltpu.VMEM)],
        out_specs=pl.BlockSpec((t, N_HEADS * KV_RANK), lambda i, s: (i, 0)),
        scratch_shapes=[pltpu.VMEM((N_HEADS, 1, t), F32),
                        pltpu.VMEM((N_HEADS, 1, t), F32),
                        pltpu.VMEM((N_HEADS, KV_RANK, t), F32)],
    )
    return pl.pallas_call(
        functools.partial(_prompt_attn_kernel, t=t),
        grid_spec=grid_spec,
        out_shape=jax.ShapeDtypeStruct((seq, N_HEADS * KV_RANK), BF16),
        compiler_params=pltpu.CompilerParams(
            dimension_semantics=("arbitrary",), vmem_limit_bytes=VMEM_LIMIT),
    )(safe, qcat, kcat, vt)


def _epilogue(o, sz, ga, yp, x, wts, tm):
    n_rows, d_model = x.shape
    rows = pl.BlockSpec((tm, d_model), lambda i: (i, 0))
    ws = [wts[k] for k in ("w_uv", "w_oatt", "w_out", "g_final")]
    return pl.pallas_call(
        _epilogue_kernel,
        grid=(n_rows // tm,),
        in_specs=[rows] * 5 + [_const_spec(w.shape) for w in ws],
        out_specs=rows,
        out_shape=jax.ShapeDtypeStruct((n_rows, d_model), F32),
        compiler_params=pltpu.CompilerParams(
            dimension_semantics=("arbitrary",), vmem_limit_bytes=VMEM_LIMIT),
    )(o, sz, ga, yp, x, *ws)


def _sample_proj(x, cs, hist, wts, past_len):
    b, d_model = x.shape
    ws = [wts[k] for k in _PROJ_WEIGHTS]
    wide = jax.ShapeDtypeStruct((b, d_model), BF16)
    out_shape = (
        jax.ShapeDtypeStruct((b, N_HEADS * LANES), F32),
        jax.ShapeDtypeStruct((b, N_HEADS * LANES), F32),
        jax.ShapeDtypeStruct((b, KV_RANK), F32),
        jax.ShapeDtypeStruct((b, QK_ROPE), F32),
        wide, wide, wide,
        jax.ShapeDtypeStruct(hist.shape, F32),
    )
    args = (x, cs, hist, *ws)
    return pl.pallas_call(
        functools.partial(_sample_proj_kernel, past_len=past_len),
        grid=(1,),
        in_specs=[_const_spec(a.shape) for a in args],
        out_specs=tuple(_const_spec(s.shape) for s in out_shape),
        out_shape=out_shape,
        compiler_params=pltpu.CompilerParams(
            dimension_semantics=("arbitrary",), vmem_limit_bytes=VMEM_LIMIT),
    )(*args)


def _sample_attn(page_table, qat, qrt, knew, rnew, cache_ckv, cache_krope):
    b, n_pages = page_table.shape
    page = cache_ckv.shape[1]
    per_step = 2
    assert b % per_step == 0 and b >= DECODE_SLOTS and n_pages % 2 == 0
    pair = lambda rows, width: pl.BlockSpec((per_step, rows, width), lambda i, pt: (i, 0, 0))
    kbuf = pltpu.VMEM((DECODE_SLOTS, n_pages // 2, 2, page, KV_RANK), F32)
    rbuf = pltpu.VMEM((DECODE_SLOTS, n_pages // 2, 2 * QK_ROPE, page), F32)
    grid_spec = pltpu.PrefetchScalarGridSpec(
        num_scalar_prefetch=1,
        grid=(b // per_step,),
        in_specs=[pair(KV_RANK, LANES), pair(QK_ROPE, LANES), pair(1, KV_RANK), pair(1, QK_ROPE),
                  pl.BlockSpec(memory_space=pl.ANY),
                  pl.BlockSpec(memory_space=pl.ANY)],
        out_specs=pair(N_HEADS, KV_RANK),
        scratch_shapes=[kbuf, rbuf, pltpu.SemaphoreType.DMA((2, DECODE_SLOTS))],
    )
    return pl.pallas_call(
        functools.partial(_sample_attn_kernel, n_pages=n_pages, per_step=per_step),
        grid_spec=grid_spec,
        out_shape=jax.ShapeDtypeStruct((b, N_HEADS, KV_RANK), F32),
        compiler_params=pltpu.CompilerParams(
            dimension_semantics=("arbitrary",), vmem_limit_bytes=VMEM_LIMIT),
    )(page_table.reshape(-1), qat, qrt, knew, rnew, cache_ckv, jnp.swapaxes(cache_krope, 1, 2))


def kernel(x_prompt, x_sample, cache_ckv, cache_krope, state_pool, page_table, g_pre, w_in, g_q,
           w_uq, g_kv, w_uk, w_uv, w_o_att, w_pool, pool_scale, w_o_pool, w_out, g_final):
    depth = g_pre.shape[0]
    batch, seq, d_model = x_prompt.shape
    dec_b, dec_seq, _ = x_sample.shape
    assert depth == 1 and batch == 1 and dec_seq == 1
    past_len = page_table.shape[1] * cache_ckv.shape[2]
    wts = _prep_weights(g_pre[0], w_in[0], g_q[0], w_uq[0], g_kv[0], w_uk[0], w_uv[0], w_o_att[0],
                        w_pool[0], pool_scale[0], w_o_pool[0], w_out[0], g_final)

    tile = 512
    proj_tile = 512
    hist_rows = 16
    xp = x_prompt.reshape(seq, d_model)
    rope_a, rope_b = _rope_factors(seq)
    hist0 = jnp.zeros((hist_rows, POOL_WIDTH), F32)
    qcat, kcat, vt, ckv_p, krt_p, sz, ga, yp, pstate, slack = _prompt_proj(
        xp, rope_a, rope_b, hist0, wts, proj_tile, tile)
    slack = jnp.max(slack[:, 0, 0].reshape(seq // tile, tile // proj_tile), axis=1)
    safe = (slack <= SAFE_SLACK).astype(jnp.int32)
    o_p = _prompt_attn(safe, qcat, kcat, vt, tile)
    y_p = _epilogue(o_p, sz, ga, yp, xp, wts, tile)

    xs = x_sample.reshape(dec_b, d_model)
    cs_s = _rope_table(jnp.full((dec_b,), past_len, jnp.int32))
    hist_s = jnp.transpose(state_pool[0], (1, 0, 2))
    qa, qr, ckv_s, kr_s, sz_s, ga_s, yp_s, pstate_s = _sample_proj(xs, cs_s, hist_s, wts, past_len)
    qa3 = qa.reshape(dec_b, N_HEADS, LANES)
    qr3 = qr.reshape(dec_b, N_HEADS, LANES)[:, :, :QK_ROPE]
    qat = jnp.tile(jnp.transpose(qa3, (0, 2, 1)), (1, 1, KEYS_PER_ROW)).astype(BF16)
    qrt = jnp.tile(jnp.transpose(qr3, (0, 2, 1)), (1, 1, KEYS_PER_ROW)).astype(BF16)
    o_s = _sample_attn(page_table, qat, qrt, ckv_s.reshape(dec_b, 1, KV_RANK),
                       kr_s.reshape(dec_b, 1, QK_ROPE), cache_ckv[0], cache_krope[0])
    o_s = o_s.reshape(dec_b, N_HEADS * KV_RANK).astype(BF16)
    y_s = _epilogue(o_s, sz_s, ga_s, yp_s, xs, wts, dec_b)

    return (y_p.reshape(batch, seq, d_model),
            y_s.reshape(dec_b, 1, d_model),
            ckv_p.reshape(1, batch, seq, KV_RANK),
            krt_p.T.reshape(1, batch, seq, QK_ROPE),
            pstate[hist_rows - POOL_HIST:].reshape(1, batch, POOL_HIST, POOL_WIDTH),
            ckv_s.reshape(1, dec_b, 1, KV_RANK),
            kr_s.reshape(1, dec_b, 1, QK_ROPE),
            jnp.transpose(pstate_s, (1, 0, 2)).reshape(1, dec_b, POOL_HIST, POOL_WIDTH))
```

```python
import functools

import jax
import jax.numpy as jnp
from jax import lax
from jax.experimental import pallas as pl
from jax.experimental.pallas import tpu as pltpu

F32 = jnp.float32
BF16 = jnp.bfloat16

N_HEADS = 8
QK_NOPE = 128
QK_ROPE = 64
V_DIM = 128
Q_RANK = 256
KV_RANK = 128
ROPE_BASE = 10000.0
POOL_WINDOWS = (2, 4, 8, 16)
POOL_GROUP_DIM = 128
POOL_WIDTH = len(POOL_WINDOWS) * POOL_GROUP_DIM
POOL_HIST = 15
EPS = 1e-6
NEG_INF = -1e30
SOFTMAX_SCALE = (QK_NOPE + QK_ROPE) ** -0.5
LOG2_E = 1.4426950408889634
Q_SCALE = SOFTMAX_SCALE * LOG2_E
SAFE_SLACK = 64.0

LANES = 128
KEYS_PER_ROW = LANES // N_HEADS
DECODE_SLOTS = 3
QCAT = 2 * LANES
VMEM_LIMIT = 56 * 1024 * 1024


def _dot(a, b):
    return jnp.dot(a, b, preferred_element_type=F32)


def _dot_nt(a, b):
    return lax.dot_general(a, b, (((1,), (1,)), ((), ())), preferred_element_type=F32)


def _rms(x, g):
    return x * lax.rsqrt(jnp.mean(x * x, axis=-1, keepdims=True) + EPS) * g


def _silu(x):
    return x * jax.nn.sigmoid(x)


def _rope_pair(t):
    return t + pltpu.roll(t, QK_ROPE, axis=1)


def _query_key_inputs(za, cs, g_q, w_qn, w_qr, g_kv):
    qn = _rms(za[:, :Q_RANK], g_q).astype(BF16)
    ckv = _rms(za[:, Q_RANK:Q_RANK + KV_RANK], g_kv)
    kr2 = _rope_pair(za[:, Q_RANK + KV_RANK:] * cs)
    q_nope = _dot(qn, w_qn)
    q_rope = _dot(qn, w_qr)
    return q_nope, q_rope, ckv, kr2


def _head_query(q_nope, q_rope, cs, w_ukt, hd):
    sl = slice(hd * LANES, (hd + 1) * LANES)
    qa = _dot(q_nope[:, sl].astype(BF16), w_ukt[hd]) * Q_SCALE
    qr = _rope_pair(q_rope[:, sl] * cs) * Q_SCALE
    return qa, qr


def _queries_keys(za, cs, g_q, w_qn, w_qr, w_ukt, g_kv):
    q_nope, q_rope, ckv, kr2 = _query_key_inputs(za, cs, g_q, w_qn, w_qr, g_kv)
    heads = [_head_query(q_nope, q_rope, cs, w_ukt, hd) for hd in range(N_HEADS)]
    return [qa for qa, _ in heads], [qr for _, qr in heads], ckv, kr2


def _pool_branch(win_sum, u, rcnt, z_pool, g_pool, w_pool, pool_scale, w_o_pool):
    mixed = []
    for g in range(len(POOL_WINDOWS)):
        sl = slice(g * POOL_GROUP_DIM, (g + 1) * POOL_GROUP_DIM)
        pooled = win_sum[g] * rcnt[g] - u[:, sl]
        mixed.append(_dot(pooled.astype(BF16), w_pool[g]))
    p = jnp.concatenate(mixed, axis=1) * pool_scale
    y_pool = _dot((p * _silu(z_pool)).astype(BF16), w_o_pool)
    return jax.nn.sigmoid(g_pool) * y_pool


def _gates_and_pool_inputs(h, w_up_ref, w_zp_ref, w_gp_ref, w_za_ref, w_ga_ref, sz_ref, ga_ref):
    sz_ref[...] = _silu(_dot_nt(h, w_za_ref[...])).astype(sz_ref.dtype)
    ga_ref[...] = jax.nn.sigmoid(_dot_nt(h, w_ga_ref[...])).astype(ga_ref.dtype)
    u = _dot_nt(h, w_up_ref[...])
    return u, _dot_nt(h, w_zp_ref[...]), _dot_nt(h, w_gp_ref[...])


def _prompt_proj_kernel(x_ref, rope_a_ref, rope_b_ref, hist_ref, g_pre_ref, w_a_ref, w_za_ref, w_up_ref,
                        w_zp_ref, w_ga_ref, w_gp_ref, g_q_ref, w_qn_ref, w_qr_ref, w_ukt_ref,
                        g_kv_ref, w_pool_ref, pscale_ref, w_opool_ref,
                        qcat_ref, kcat_ref, vt_ref, ckv_ref, kr_ref, sz_ref, ga_ref, yp_ref,
                        pstate_ref, slack_ref, uext_ref, kmax_ref, *, tm):
    i = pl.program_id(0)
    hist_rows = uext_ref.shape[0] - tm

    @pl.when(i == 0)
    def _():
        uext_ref[0:hist_rows, :] = hist_ref[...]
        kmax_ref[...] = jnp.zeros(kmax_ref.shape, F32)

    cs = jnp.concatenate(
        [rope_a_ref[a, 0:1, :] * rope_b_ref[0] + rope_a_ref[a, 1:2, :] * rope_b_ref[1]
         for a in range(rope_a_ref.shape[0])], axis=0)
    h = _rms(x_ref[...], g_pre_ref[...]).astype(BF16)
    za = _dot_nt(h, w_a_ref[...])
    q_nope, q_rope, ckv, kr2 = _query_key_inputs(za, cs, g_q_ref[...], w_qn_ref[...], w_qr_ref[...],
                                                 g_kv_ref[...])
    lane = lax.broadcasted_iota(jnp.int32, kr2.shape, 1)
    is_rope = lane < QK_ROPE
    is_shift = lane == QK_ROPE
    rsum = lambda a: jnp.sum(a, axis=1, keepdims=True)
    kr_m = jnp.where(is_rope, kr2, 0.0)
    k_norm2 = jnp.max(rsum(ckv * ckv + kr_m * kr_m), axis=0, keepdims=True)
    k_norm2 = jnp.maximum(k_norm2, kmax_ref[0:1, 0:1])
    kmax_ref[...] = jnp.broadcast_to(k_norm2, kmax_ref.shape)
    half = w_za_ref.shape[0] // 2
    side_val = {}

    def gate_chunk(out_ref, w_ref, act, c):
        def run():
            out_ref[:, c * half:(c + 1) * half] = act(
                _dot_nt(h, w_ref[c * half:(c + 1) * half, :])).astype(out_ref.dtype)
        return run

    def keep(name, w_ref, rows):
        def run():
            side_val[name] = _dot_nt(h, w_ref[rows, :])
        return run

    side = [gate_chunk(sz_ref, w_za_ref, _silu, 0), gate_chunk(sz_ref, w_za_ref, _silu, 1),
            gate_chunk(ga_ref, w_ga_ref, jax.nn.sigmoid, 0), gate_chunk(ga_ref, w_ga_ref, jax.nn.sigmoid, 1),
            keep("u", w_up_ref, slice(None)), keep("z_pool", w_zp_ref, slice(None)),
            keep("g_pool0", w_gp_ref, slice(0, half)), keep("g_pool1", w_gp_ref, slice(half, 2 * half))]
    assert len(side) == N_HEADS
    slack = None
    for hd in range(N_HEADS):
        qa, qr = _head_query(q_nope, q_rope, cs, w_ukt_ref, hd)
        side[hd]()
        qr_m = jnp.where(is_rope, qr, 0.0)
        c = jnp.sqrt(rsum(qa * qa + qr_m * qr_m) * k_norm2) * 1.01 + 1.0
        own = rsum(qa * ckv + qr_m * kr_m)
        worst = jnp.max(c - own, axis=0, keepdims=True)
        slack = worst if slack is None else jnp.maximum(slack, worst)
        qcat_ref[hd, :, 0:LANES] = qa.astype(BF16)
        qcat_ref[hd, :, LANES:QCAT] = jnp.where(is_shift, -c, qr_m).astype(BF16)
    slack_ref[0] = jnp.broadcast_to(slack, slack_ref.shape[1:])
    kcat_ref[:, 0:LANES] = ckv.astype(BF16)
    kcat_ref[:, LANES:QCAT] = jnp.where(is_shift, 1.0, kr_m).astype(BF16)
    vt_ref[0] = ckv.T.astype(BF16)
    ckv_ref[...] = ckv
    kr_ref[...] = kr2.T[:QK_ROPE, :]

    u, z_pool = side_val["u"], side_val["z_pool"]
    g_pool = jnp.concatenate([side_val["g_pool0"], side_val["g_pool1"]], axis=1)
    uext_ref[hist_rows:, :] = u
    pos = i * tm + lax.broadcasted_iota(jnp.int32, (tm, 1), 0)
    win_sum, rcnt = [], []
    for g, w in enumerate(POOL_WINDOWS):
        c0 = g * POOL_GROUP_DIM
        acc = u[:, c0:c0 + POOL_GROUP_DIM]
        for s in range(1, w):
            acc = acc + uext_ref[hist_rows - s:hist_rows - s + tm, c0:c0 + POOL_GROUP_DIM]
        win_sum.append(acc)
        rcnt.append(1.0 / jnp.minimum(w, pos + 1).astype(F32))
    yp_ref[...] = _pool_branch(win_sum, u, rcnt, z_pool, g_pool, w_pool_ref, pscale_ref[...],
                               w_opool_ref[...]).astype(yp_ref.dtype)
    tail = uext_ref[tm:tm + hist_rows, :]
    pstate_ref[...] = tail
    uext_ref[0:hist_rows, :] = tail


def _prompt_attn_kernel(safe_ref, q_ref, k_ref, vt_ref, o_ref, m_sc, l_sc, acc_sc, *, t):
    i = pl.program_id(0)
    l_sc[...] = jnp.zeros(l_sc.shape, F32)
    acc_sc[...] = jnp.zeros(acc_sc.shape, F32)

    def chunks(specs, shifted):
        kcs = [k_ref[pl.ds(pl.multiple_of(j * t, t), t), :] for j, _ in specs]
        vts = [vt_ref[j] for j, _ in specs]
        units = [(c, hd) for c in range(len(specs)) for hd in range(N_HEADS)]
        if any(masked for _, masked in specs):
            key = lax.broadcasted_iota(jnp.int32, (t, t), 0)
            qry = lax.broadcasted_iota(jnp.int32, (t, t), 1)
            keep = key <= qry
        scores = [_dot_nt(kcs[c], q_ref[hd]) for c, hd in units]
        for (c, hd), st in zip(units, scores):
            if specs[c][1]:
                st = jnp.where(keep, st, NEG_INF)
            if shifted:
                p = jnp.exp2(st)
                l_sc[hd] += jnp.sum(p, axis=0, keepdims=True)
                acc_sc[hd] += _dot(vts[c], p.astype(BF16))
            else:
                m_prev = m_sc[hd]
                m_new = jnp.maximum(m_prev, jnp.max(st, axis=0, keepdims=True))
                alpha = jnp.exp2(m_prev - m_new)
                p = jnp.exp2(st - m_new)
                l_sc[hd] = alpha * l_sc[hd] + jnp.sum(p, axis=0, keepdims=True)
                acc_sc[hd] = alpha * acc_sc[hd] + _dot(vts[c], p.astype(BF16))
                m_sc[hd] = m_new

    def sweep(shifted, width):
        def body(jw, carry):
            chunks([(jw * width + c, False) for c in range(width)], shifted)
            return carry
        lax.fori_loop(0, i // width, body, 0)
        for rem in range(width):
            @pl.when(i % width == rem)
            def _(rem=rem):
                chunks([(i - rem + c, False) for c in range(rem)] + [(i, True)], shifted)

    safe = safe_ref[i] != 0

    @pl.when(safe)
    def _():
        sweep(True, 3)

    @pl.when(jnp.logical_not(safe))
    def _():
        m_sc[...] = jnp.full(m_sc.shape, NEG_INF, F32)
        sweep(False, 1)

    for hd in range(N_HEADS):
        o = acc_sc[hd] / l_sc[hd]
        o_ref[:, hd * KV_RANK:(hd + 1) * KV_RANK] = o.T.astype(BF16)


def _epilogue_kernel(o_ref, sz_ref, ga_ref, yp_ref, x_ref, w_uv_ref, w_oatt_ref, w_out_ref,
                     g_final_ref, y_ref):
    o = o_ref[...]
    o2 = jnp.concatenate(
        [_dot(o[:, hd * KV_RANK:(hd + 1) * KV_RANK], w_uv_ref[hd]) for hd in range(N_HEADS)], axis=1)
    y_att = _dot((o2 * sz_ref[...].astype(F32)).astype(BF16), w_oatt_ref[...])
    merged = ga_ref[...].astype(F32) * y_att + yp_ref[...].astype(F32)
    xo = x_ref[...] + _dot(merged.astype(BF16), w_out_ref[...])
    y_ref[...] = _rms(xo, g_final_ref[...])


def _sample_proj_kernel(x_ref, cs_ref, hist_ref, g_pre_ref, w_a_ref, w_za_ref, w_up_ref,
                        w_zp_ref, w_ga_ref, w_gp_ref, g_q_ref, w_qn_ref, w_qr_ref, w_ukt_ref,
                        g_kv_ref, w_pool_ref, pscale_ref, w_opool_ref,
                        qa_ref, qr_ref, ckv_ref, kr_ref, sz_ref, ga_ref, yp_ref, pstate_ref,
                        *, past_len):
    h = _rms(x_ref[...], g_pre_ref[...]).astype(BF16)
    za = _dot_nt(h, w_a_ref[...])
    qa, qr, ckv, kr2 = _queries_keys(za, cs_ref[...], g_q_ref[...], w_qn_ref[...], w_qr_ref[...],
                                     w_ukt_ref, g_kv_ref[...])
    for hd in range(N_HEADS):
        qa_ref[:, hd * LANES:(hd + 1) * LANES] = qa[hd]
        qr_ref[:, hd * LANES:(hd + 1) * LANES] = qr[hd]
    ckv_ref[...] = ckv
    kr_ref[...] = kr2[:, :QK_ROPE]

    u, z_pool, g_pool = _gates_and_pool_inputs(h, w_up_ref, w_zp_ref, w_gp_ref, w_za_ref, w_ga_ref,
                                               sz_ref, ga_ref)
    win_sum, rcnt = [], []
    for g, w in enumerate(POOL_WINDOWS):
        c0 = g * POOL_GROUP_DIM
        acc = u[:, c0:c0 + POOL_GROUP_DIM]
        for s in range(1, w):
            acc = acc + hist_ref[POOL_HIST - s, :, c0:c0 + POOL_GROUP_DIM]
        win_sum.append(acc)
        rcnt.append(1.0 / float(min(w, past_len + 1)))
    yp_ref[...] = _pool_branch(win_sum, u, rcnt, z_pool, g_pool, w_pool_ref, pscale_ref[...],
                               w_opool_ref[...]).astype(yp_ref.dtype)
    pstate_ref[0:POOL_HIST - 1] = hist_ref[1:POOL_HIST]
    pstate_ref[POOL_HIST - 1] = u


def _lane_group_reduce(v, op):
    sh = N_HEADS
    while sh < LANES:
        v = op(v, pltpu.roll(v, sh, axis=1))
        sh *= 2
    return v


def _diag_to_col(v):
    sub = lax.broadcasted_iota(jnp.int32, v.shape, 0)
    lane = lax.broadcasted_iota(jnp.int32, v.shape, 1)
    return jnp.sum(jnp.where(sub == lane, v, 0.0), axis=1, keepdims=True)


def _latent_rows(kbuf):
    n_pairs, _, page, width = kbuf.shape
    halves = []
    for half in range(2):
        cols = []
        for jj in range(KEYS_PER_ROW // 2):
            r0 = (half * (KEYS_PER_ROW // 2) + jj) * 8
            for par in range(2):
                cols.append(kbuf[:, par, r0:r0 + 8, :].reshape(n_pairs * 8, width))
        halves.append(jnp.concatenate(cols, axis=1))
    return jnp.concatenate(halves, axis=0).astype(BF16)


def _rotary_rows(rbuf):
    n_pairs = rbuf.shape[0]
    xt = jnp.swapaxes(rbuf[...], 1, 2)
    halves = []
    for half in range(2):
        cols = []
        for jj in range(KEYS_PER_ROW // 2):
            r0 = (half * (KEYS_PER_ROW // 2) + jj) * 8
            cols.append(xt[:, r0:r0 + 8, :].reshape(n_pairs * 8, LANES))
        halves.append(jnp.concatenate(cols, axis=1))
    return jnp.concatenate(halves, axis=0).astype(BF16)


def _attend_one(qat, qrt, knew, rnew, kbuf, rbuf):
    lane_a = lax.broadcasted_iota(jnp.int32, qat.shape, 1) // N_HEADS
    lane_r = lax.broadcasted_iota(jnp.int32, qrt.shape, 1) // N_HEADS
    zero = jnp.zeros((), BF16)
    wk = jnp.concatenate([jnp.where(lane_a == j, qat, zero) for j in range(KEYS_PER_ROW)], axis=0)
    wr = jnp.concatenate([jnp.where(lane_r == j, qrt, zero) for j in range(KEYS_PER_ROW)], axis=0)

    kb = _latent_rows(kbuf)
    rb = _rotary_rows(rbuf)
    hr = kb.shape[0] // 2
    s_lat = [_dot(kb[:hr], wk), _dot(kb[hr:], wk)]
    s_rot = [_dot(rb[:hr], wr), _dot(rb[hr:], wr)]
    s2 = jnp.concatenate([s_lat[0] + s_rot[0], s_lat[1] + s_rot[1]], axis=0)

    knew_b = knew.astype(BF16)
    rnew_b = rnew.astype(BF16)
    s_new = (_dot(jnp.broadcast_to(knew_b, (8, KV_RANK)), qat)
             + _dot(jnp.broadcast_to(rnew_b, (8, QK_ROPE)), qrt))

    m = jnp.max(s2, axis=0, keepdims=True)
    m = _lane_group_reduce(jnp.broadcast_to(m, (8, LANES)), jnp.maximum)
    m = jnp.maximum(m, s_new)
    p2 = jnp.exp2(s2 - m[0:1])
    p_new = jnp.exp2(s_new - m)
    l_lane = _lane_group_reduce(
        jnp.broadcast_to(jnp.sum(p2, axis=0, keepdims=True), (8, LANES)), jnp.add)
    p_new_col = _diag_to_col(p_new)
    l_col = _diag_to_col(l_lane) + p_new_col

    p2t = p2.T.astype(BF16)
    hs = KEYS_PER_ROW // 2
    hp, hk = hs * N_HEADS, hs * KV_RANK
    g_halves = [_dot(p2t[:hp], kb[:, :hk]), _dot(p2t[hp:], kb[:, hk:])]
    o = p_new_col.astype(BF16).astype(F32) * knew_b.astype(F32)
    for g in g_halves:
        for j in range(hs):
            o = o + g[j * N_HEADS:(j + 1) * N_HEADS, j * KV_RANK:(j + 1) * KV_RANK]
    return o / l_col


def _sample_attn_kernel(pt_ref, qat_ref, qrt_ref, knew_ref, rnew_ref, ck_hbm, krt_hbm, o_ref,
                        kbuf, rbuf, sems, *, n_pages, per_step):
    g = pl.program_id(0)
    n_seq = pl.num_programs(0) * per_step

    def start_pair(seq, slot, q):
        for par in range(2):
            pid = pt_ref[seq * n_pages + 2 * q + par]
            pltpu.make_async_copy(ck_hbm.at[pid], kbuf.at[slot, q, par], sems.at[0, slot]).start()
            pltpu.make_async_copy(
                krt_hbm.at[pid], rbuf.at[slot, q, par * QK_ROPE:(par + 1) * QK_ROPE, :],
                sems.at[1, slot]).start()

    def wait_all(slot):
        pltpu.make_async_copy(kbuf.at[slot], kbuf.at[slot], sems.at[0, slot]).wait()
        pltpu.make_async_copy(rbuf.at[slot], rbuf.at[slot], sems.at[1, slot]).wait()

    @pl.when(g == 0)
    def _():
        def body(q, c):
            for slot in range(DECODE_SLOTS):
                start_pair(slot, slot, q)
            return c
        lax.fori_loop(0, n_pages // 2, body, 0)

    for k in range(per_step):
        seq = g * per_step + k
        slot = seq % DECODE_SLOTS
        wait_all(slot)
        o_ref[k] = _attend_one(qat_ref[k], qrt_ref[k], knew_ref[k], rnew_ref[k],
                               kbuf.at[slot], rbuf.at[slot])
        refill = jnp.minimum(seq + DECODE_SLOTS, n_seq - 1)
        for q in range(n_pages // 2):
            start_pair(refill, slot, q)

    @pl.when(g == pl.num_programs(0) - 1)
    def _():
        for slot in range(DECODE_SLOTS):
            wait_all(slot)


def _const_spec(shape):
    zeros = (0,) * len(shape)
    return pl.BlockSpec(shape, lambda *_: zeros)


def _prep_weights(g_pre, w_in, g_q, w_uq, g_kv, w_uk, w_uv, w_o_att, w_pool, pool_scale,
                  w_o_pool, w_out, g_final):
    d_model = w_in.shape[0]
    att_w = N_HEADS * V_DIM
    half = QK_ROPE // 2
    o_kv = Q_RANK
    o_kr = o_kv + KV_RANK
    o_za = o_kr + QK_ROPE
    o_up = o_za + att_w
    o_zp = o_up + POOL_WIDTH
    o_ga = o_zp + POOL_WIDTH
    o_gp = o_ga + d_model
    w_t = w_in.T
    w_kr = w_t[o_kr:o_za]
    w_a = jnp.concatenate([w_t[:o_kr], w_kr, -w_kr[half:], w_kr[:half]], axis=0)
    uq = w_uq.reshape(Q_RANK, N_HEADS, QK_NOPE + QK_ROPE)
    w_qn = uq[:, :, :QK_NOPE].reshape(Q_RANK, N_HEADS * QK_NOPE)
    x1, x2 = uq[:, :, QK_NOPE:QK_NOPE + half], uq[:, :, QK_NOPE + half:]
    w_qr = jnp.concatenate([x1, x2, -x2, x1], axis=2).reshape(Q_RANK, N_HEADS * LANES)
    bf = lambda a: a.astype(BF16)
    row = lambda a: a.reshape(1, -1).astype(F32)
    return dict(
        g_pre=row(g_pre), w_a=bf(w_a), w_za=bf(w_t[o_za:o_up]), w_up=bf(w_t[o_up:o_zp]),
        w_zp=bf(w_t[o_zp:o_ga]), w_ga=bf(w_t[o_ga:o_gp]), w_gp=bf(w_t[o_gp:]),
        g_q=row(g_q), w_qn=bf(w_qn), w_qr=bf(w_qr),
        w_ukt=bf(jnp.transpose(w_uk, (1, 2, 0))),
        g_kv=row(g_kv), w_pool=bf(w_pool), pscale=row(pool_scale), w_opool=bf(w_o_pool),
        w_uv=bf(jnp.transpose(w_uv, (1, 0, 2))),
        w_oatt=bf(w_o_att), w_out=bf(w_out), g_final=row(g_final))


_PROJ_WEIGHTS = ("g_pre", "w_a", "w_za", "w_up", "w_zp", "w_ga", "w_gp", "g_q", "w_qn", "w_qr",
                 "w_ukt", "g_kv", "w_pool", "pscale", "w_opool")


def _cos_sin(pos):
    half = QK_ROPE // 2
    inv = ROPE_BASE ** (-jnp.arange(half, dtype=F32) / half)
    ang = pos.astype(F32)[:, None] * inv[None, :]
    return jnp.cos(ang), jnp.sin(ang)


def _rope_table(pos):
    c, s = _cos_sin(pos)
    return jnp.concatenate([c, c, s, s], axis=1)


ROPE_BLOCK = 128


def _rope_factors(seq):
    ca, sa = _cos_sin(jnp.arange(0, seq, ROPE_BLOCK, dtype=jnp.int32))
    cb, sb = _cos_sin(jnp.arange(ROPE_BLOCK, dtype=jnp.int32))
    rope_a = jnp.stack([jnp.concatenate([ca, ca, sa, sa], axis=1),
                        jnp.concatenate([-sa, -sa, ca, ca], axis=1)], axis=1)
    rope_b = jnp.stack([jnp.concatenate([cb] * 4, axis=1),
                        jnp.concatenate([sb] * 4, axis=1)], axis=0)
    return rope_a, rope_b


def _prompt_proj(x, rope_a, rope_b, hist, wts, tm, t_attn):
    seq, d_model = x.shape
    n = seq // tm
    per_attn = t_attn // tm
    blocks = tm // ROPE_BLOCK
    hist_rows = hist.shape[0]
    ws = [wts[k] for k in _PROJ_WEIGHTS]
    rows = lambda width: pl.BlockSpec((tm, width), lambda i: (i, 0))
    out_shape = (
        jax.ShapeDtypeStruct((N_HEADS, seq, QCAT), BF16),
        jax.ShapeDtypeStruct((seq, QCAT), BF16),
        jax.ShapeDtypeStruct((seq // t_attn, KV_RANK, t_attn), BF16),
        jax.ShapeDtypeStruct((seq, KV_RANK), F32),
        jax.ShapeDtypeStruct((QK_ROPE, seq), F32),
        jax.ShapeDtypeStruct((seq, d_model), BF16),
        jax.ShapeDtypeStruct((seq, d_model), BF16),
        jax.ShapeDtypeStruct((seq, d_model), BF16),
        jax.ShapeDtypeStruct((hist_rows, POOL_WIDTH), F32),
        jax.ShapeDtypeStruct((n, 8, LANES), F32),
    )
    out_specs = (
        pl.BlockSpec((N_HEADS, tm, QCAT), lambda i: (0, i, 0)),
        rows(QCAT),
        pl.BlockSpec((1, KV_RANK, tm), lambda i: (i // per_attn, 0, i % per_attn)),
        rows(KV_RANK), pl.BlockSpec((QK_ROPE, tm), lambda i: (0, i)),
        rows(d_model), rows(d_model), rows(d_model),
        _const_spec((hist_rows, POOL_WIDTH)),
        pl.BlockSpec((1, 8, LANES), lambda i: (i, 0, 0)),
    )
    return pl.pallas_call(
        functools.partial(_prompt_proj_kernel, tm=tm),
        grid=(n,),
        in_specs=[rows(d_model), pl.BlockSpec((blocks, 2, LANES), lambda i: (i, 0, 0)),
                  _const_spec(rope_b.shape), _const_spec(hist.shape)]
        + [_const_spec(w.shape) for w in ws],
        out_specs=out_specs,
        out_shape=out_shape,
        scratch_shapes=[pltpu.VMEM((hist_rows + tm, POOL_WIDTH), F32),
                        pltpu.VMEM((8, LANES), F32)],
        compiler_params=pltpu.CompilerParams(
            dimension_semantics=("arbitrary",), vmem_limit_bytes=VMEM_LIMIT),
    )(x, rope_a, rope_b, hist, *ws)


def _prompt_attn(safe, qcat, kcat, vt, t):
    seq = kcat.shape[0]
    n = seq // t
    grid_spec = pltpu.PrefetchScalarGridSpec(
        num_scalar_prefetch=1,
        grid=(n,),
        in_specs=[pl.BlockSpec((N_HEADS, t, QCAT), lambda i, s: (0, i, 0)),
                  pl.BlockSpec(memory_space=pltpu.VMEM),
                  pl.BlockSpec(memory_space=pltpu.VMEM)],
        out_specs=pl.BlockSpec((t, N_HEADS * KV_RANK), lambda i, s: (i, 0)),
        scratch_shapes=[pltpu.VMEM((N_HEADS, 1, t), F32),
                        pltpu.VMEM((N_HEADS, 1, t), F32),
                        pltpu.VMEM((N_HEADS, KV_RANK, t), F32)],
    )
    return pl.pallas_call(
        functools.partial(_prompt_attn_kernel, t=t),
        grid_spec=grid_spec,
        out_shape=jax.ShapeDtypeStruct((seq, N_HEADS * KV_RANK), BF16),
        compiler_params=pltpu.CompilerParams(
            dimension_semantics=("arbitrary",), vmem_limit_bytes=VMEM_LIMIT),
    )(safe, qcat, kcat, vt)


def _epilogue(o, sz, ga, yp, x, wts, tm):
    n_rows, d_model = x.shape
    rows = pl.BlockSpec((tm, d_model), lambda i: (i, 0))
    ws = [wts[k] for k in ("w_uv", "w_oatt", "w_out", "g_final")]
    return pl.pallas_call(
        _epilogue_kernel,
        grid=(n_rows // tm,),
        in_specs=[rows] * 5 + [_const_spec(w.shape) for w in ws],
        out_specs=rows,
        out_shape=jax.ShapeDtypeStruct((n_rows, d_model), F32),
        compiler_params=pltpu.CompilerParams(
            dimension_semantics=("arbitrary",), vmem_limit_bytes=VMEM_LIMIT),
    )(o, sz, ga, yp, x, *ws)


def _sample_proj(x, cs, hist, wts, past_len):
    b, d_model = x.shape
    ws = [wts[k] for k in _PROJ_WEIGHTS]
    wide = jax.ShapeDtypeStruct((b, d_model), BF16)
    out_shape = (
        jax.ShapeDtypeStruct((b, N_HEADS * LANES), F32),
        jax.ShapeDtypeStruct((b, N_HEADS * LANES), F32),
        jax.ShapeDtypeStruct((b, KV_RANK), F32),
        jax.ShapeDtypeStruct((b, QK_ROPE), F32),
        wide, wide, wide,
        jax.ShapeDtypeStruct(hist.shape, F32),
    )
    args = (x, cs, hist, *ws)
    return pl.pallas_call(
        functools.partial(_sample_proj_kernel, past_len=past_len),
        grid=(1,),
        in_specs=[_const_spec(a.shape) for a in args],
        out_specs=tuple(_const_spec(s.shape) for s in out_shape),
        out_shape=out_shape,
        compiler_params=pltpu.CompilerParams(
            dimension_semantics=("arbitrary",), vmem_limit_bytes=VMEM_LIMIT),
    )(*args)


def _sample_attn(page_table, qat, qrt, knew, rnew, cache_ckv, cache_krope):
    b, n_pages = page_table.shape
    page = cache_ckv.shape[1]
    per_step = 2
    assert b % per_step == 0 and b >= DECODE_SLOTS and n_pages % 2 == 0
    pair = lambda rows, width: pl.BlockSpec((per_step, rows, width), lambda i, pt: (i, 0, 0))
    kbuf = pltpu.VMEM((DECODE_SLOTS, n_pages // 2, 2, page, KV_RANK), F32)
    rbuf = pltpu.VMEM((DECODE_SLOTS, n_pages // 2, 2 * QK_ROPE, page), F32)
    grid_spec = pltpu.PrefetchScalarGridSpec(
        num_scalar_prefetch=1,
        grid=(b // per_step,),
        in_specs=[pair(KV_RANK, LANES), pair(QK_ROPE, LANES), pair(1, KV_RANK), pair(1, QK_ROPE),
                  pl.BlockSpec(memory_space=pl.ANY),
                  pl.BlockSpec(memory_space=pl.ANY)],
        out_specs=pair(N_HEADS, KV_RANK),
        scratch_shapes=[kbuf, rbuf, pltpu.SemaphoreType.DMA((2, DECODE_SLOTS))],
    )
    return pl.pallas_call(
        functools.partial(_sample_attn_kernel, n_pages=n_pages, per_step=per_step),
        grid_spec=grid_spec,
        out_shape=jax.ShapeDtypeStruct((b, N_HEADS, KV_RANK), F32),
        compiler_params=pltpu.CompilerParams(
            dimension_semantics=("arbitrary",), vmem_limit_bytes=VMEM_LIMIT),
    )(page_table.reshape(-1), qat, qrt, knew, rnew, cache_ckv, jnp.swapaxes(cache_krope, 1, 2))


def kernel(x_prompt, x_sample, cache_ckv, cache_krope, state_pool, page_table, g_pre, w_in, g_q,
           w_uq, g_kv, w_uk, w_uv, w_o_att, w_pool, pool_scale, w_o_pool, w_out, g_final):
    depth = g_pre.shape[0]
    batch, seq, d_model = x_prompt.shape
    dec_b, dec_seq, _ = x_sample.shape
    assert depth == 1 and batch == 1 and dec_seq == 1
    past_len = page_table.shape[1] * cache_ckv.shape[2]
    wts = _prep_weights(g_pre[0], w_in[0], g_q[0], w_uq[0], g_kv[0], w_uk[0], w_uv[0], w_o_att[0],
                        w_pool[0], pool_scale[0], w_o_pool[0], w_out[0], g_final)

    tile = 512
    proj_tile = 512
    hist_rows = 16
    xp = x_prompt.reshape(seq, d_model)
    rope_a, rope_b = _rope_factors(seq)
    hist0 = jnp.zeros((hist_rows, POOL_WIDTH), F32)
    qcat, kcat, vt, ckv_p, krt_p, sz, ga, yp, pstate, slack = _prompt_proj(
        xp, rope_a, rope_b, hist0, wts, proj_tile, tile)
    slack = jnp.max(slack[:, 0, 0].reshape(seq // tile, tile // proj_tile), axis=1)
    safe = (slack <= SAFE_SLACK).astype(jnp.int32)
    o_p = _prompt_attn(safe, qcat, kcat, vt, tile)
    y_p = _epilogue(o_p, sz, ga, yp, xp, wts, 2 * tile)

    xs = x_sample.reshape(dec_b, d_model)
    cs_s = _rope_table(jnp.full((dec_b,), past_len, jnp.int32))
    hist_s = jnp.transpose(state_pool[0], (1, 0, 2))
    qa, qr, ckv_s, kr_s, sz_s, ga_s, yp_s, pstate_s = _sample_proj(xs, cs_s, hist_s, wts, past_len)
    qa3 = qa.reshape(dec_b, N_HEADS, LANES)
    qr3 = qr.reshape(dec_b, N_HEADS, LANES)[:, :, :QK_ROPE]
    qat = jnp.tile(jnp.transpose(qa3, (0, 2, 1)), (1, 1, KEYS_PER_ROW)).astype(BF16)
    qrt = jnp.tile(jnp.transpose(qr3, (0, 2, 1)), (1, 1, KEYS_PER_ROW)).astype(BF16)
    o_s = _sample_attn(page_table, qat, qrt, ckv_s.reshape(dec_b, 1, KV_RANK),
                       kr_s.reshape(dec_b, 1, QK_ROPE), cache_ckv[0], cache_krope[0])
    o_s = o_s.reshape(dec_b, N_HEADS * KV_RANK).astype(BF16)
    y_s = _epilogue(o_s, sz_s, ga_s, yp_s, xs, wts, dec_b)

    return (y_p.reshape(batch, seq, d_model),
            y_s.reshape(dec_b, 1, d_model),
            ckv_p.reshape(1, batch, seq, KV_RANK),
            krt_p.T.reshape(1, batch, seq, QK_ROPE),
            pstate[hist_rows - POOL_HIST:].reshape(1, batch, POOL_HIST, POOL_WIDTH),
            ckv_s.reshape(1, dec_b, 1, KV_RANK),
            kr_s.reshape(1, dec_b, 1, QK_ROPE),
            jnp.transpose(pstate_s, (1, 0, 2)).reshape(1, dec_b, POOL_HIST, POOL_WIDTH))
```

```python
import functools

import jax
import jax.numpy as jnp
from jax import lax
from jax.experimental import pallas as pl
from jax.experimental.pallas import tpu as pltpu

F32 = jnp.float32
BF16 = jnp.bfloat16

N_HEADS = 8
QK_NOPE = 128
QK_ROPE = 64
V_DIM = 128
Q_RANK = 256
KV_RANK = 128
ROPE_BASE = 10000.0
POOL_WINDOWS = (2, 4, 8, 16)
POOL_GROUP_DIM = 128
POOL_WIDTH = len(POOL_WINDOWS) * POOL_GROUP_DIM
POOL_HIST = 15
EPS = 1e-6
NEG_INF = -1e30
SOFTMAX_SCALE = (QK_NOPE + QK_ROPE) ** -0.5
LOG2_E = 1.4426950408889634
Q_SCALE = SOFTMAX_SCALE * LOG2_E
SAFE_SLACK = 64.0

LANES = 128
KEYS_PER_ROW = LANES // N_HEADS
DECODE_SLOTS = 3
QCAT = 2 * LANES
VMEM_LIMIT = 56 * 1024 * 1024


def _dot(a, b):
    return jnp.dot(a, b, preferred_element_type=F32)


def _dot_nt(a, b):
    return lax.dot_general(a, b, (((1,), (1,)), ((), ())), preferred_element_type=F32)


def _rms(x, g):
    return x * lax.rsqrt(jnp.mean(x * x, axis=-1, keepdims=True) + EPS) * g


def _silu(x):
    return x * jax.nn.sigmoid(x)


def _rope_pair(t):
    return t + pltpu.roll(t, QK_ROPE, axis=1)


def _query_key_inputs(za, cs, g_q, w_qn, w_qr, g_kv):
    qn = _rms(za[:, :Q_RANK], g_q).astype(BF16)
    ckv = _rms(za[:, Q_RANK:Q_RANK + KV_RANK], g_kv)
    kr2 = _rope_pair(za[:, Q_RANK + KV_RANK:] * cs)
    q_nope = _dot(qn, w_qn)
    q_rope = _dot(qn, w_qr)
    return q_nope, q_rope, ckv, kr2


def _head_query(q_nope, q_rope, cs, w_ukt, hd):
    sl = slice(hd * LANES, (hd + 1) * LANES)
    qa = _dot(q_nope[:, sl].astype(BF16), w_ukt[hd]) * Q_SCALE
    qr = _rope_pair(q_rope[:, sl] * cs) * Q_SCALE
    return qa, qr


def _queries_keys(za, cs, g_q, w_qn, w_qr, w_ukt, g_kv):
    q_nope, q_rope, ckv, kr2 = _query_key_inputs(za, cs, g_q, w_qn, w_qr, g_kv)
    heads = [_head_query(q_nope, q_rope, cs, w_ukt, hd) for hd in range(N_HEADS)]
    return [qa for qa, _ in heads], [qr for _, qr in heads], ckv, kr2


def _pool_branch(win_sum, u, rcnt, z_pool, g_pool, w_pool, pool_scale, w_o_pool):
    mixed = []
    for g in range(len(POOL_WINDOWS)):
        sl = slice(g * POOL_GROUP_DIM, (g + 1) * POOL_GROUP_DIM)
        pooled = win_sum[g] * rcnt[g] - u[:, sl]
        mixed.append(_dot(pooled.astype(BF16), w_pool[g]))
    p = jnp.concatenate(mixed, axis=1) * pool_scale
    y_pool = _dot((p * _silu(z_pool)).astype(BF16), w_o_pool)
    return jax.nn.sigmoid(g_pool) * y_pool


def _gates_and_pool_inputs(h, w_up_ref, w_zp_ref, w_gp_ref, w_za_ref, w_ga_ref, sz_ref, ga_ref):
    sz_ref[...] = _silu(_dot_nt(h, w_za_ref[...])).astype(sz_ref.dtype)
    ga_ref[...] = jax.nn.sigmoid(_dot_nt(h, w_ga_ref[...])).astype(ga_ref.dtype)
    u = _dot_nt(h, w_up_ref[...])
    return u, _dot_nt(h, w_zp_ref[...]), _dot_nt(h, w_gp_ref[...])


def _prompt_proj_kernel(x_ref, rope_a_ref, rope_b_ref, hist_ref, g_pre_ref, w_a_ref, w_za_ref, w_up_ref,
                        w_zp_ref, w_ga_ref, w_gp_ref, g_q_ref, w_qn_ref, w_qr_ref, w_ukt_ref,
                        g_kv_ref, w_pool_ref, pscale_ref, w_opool_ref,
                        qcat_ref, kcat_ref, vt_ref, ckv_ref, kr_ref, sz_ref, ga_ref, yp_ref,
                        pstate_ref, slack_ref, uext_ref, kmax_ref, *, tm):
    i = pl.program_id(0)
    hist_rows = uext_ref.shape[0] - tm

    @pl.when(i == 0)
    def _():
        uext_ref[0:hist_rows, :] = hist_ref[...]
        kmax_ref[...] = jnp.zeros(kmax_ref.shape, F32)

    cs = jnp.concatenate(
        [rope_a_ref[a, 0:1, :] * rope_b_ref[0] + rope_a_ref[a, 1:2, :] * rope_b_ref[1]
         for a in range(rope_a_ref.shape[0])], axis=0)
    h = _rms(x_ref[...], g_pre_ref[...]).astype(BF16)
    za = _dot_nt(h, w_a_ref[...])
    q_nope, q_rope, ckv, kr2 = _query_key_inputs(za, cs, g_q_ref[...], w_qn_ref[...], w_qr_ref[...],
                                                 g_kv_ref[...])
    lane = lax.broadcasted_iota(jnp.int32, kr2.shape, 1)
    is_rope = lane < QK_ROPE
    is_shift = lane == QK_ROPE
    rsum = lambda a: jnp.sum(a, axis=1, keepdims=True)
    kr_m = jnp.where(is_rope, kr2, 0.0)
    k_norm2 = jnp.max(rsum(ckv * ckv + kr_m * kr_m), axis=0, keepdims=True)
    k_norm2 = jnp.maximum(k_norm2, kmax_ref[0:1, 0:1])
    kmax_ref[...] = jnp.broadcast_to(k_norm2, kmax_ref.shape)
    half = w_za_ref.shape[0] // 2
    side_val = {}

    def gate_chunk(out_ref, w_ref, act, c):
        def run():
            out_ref[:, c * half:(c + 1) * half] = act(
                _dot_nt(h, w_ref[c * half:(c + 1) * half, :])).astype(out_ref.dtype)
        return run

    def keep(name, w_ref, rows):
        def run():
            side_val[name] = _dot_nt(h, w_ref[rows, :])
        return run

    side = [gate_chunk(sz_ref, w_za_ref, _silu, 0), gate_chunk(sz_ref, w_za_ref, _silu, 1),
            gate_chunk(ga_ref, w_ga_ref, jax.nn.sigmoid, 0), gate_chunk(ga_ref, w_ga_ref, jax.nn.sigmoid, 1),
            keep("u", w_up_ref, slice(None)), keep("z_pool", w_zp_ref, slice(None)),
            keep("g_pool0", w_gp_ref, slice(0, half)), keep("g_pool1", w_gp_ref, slice(half, 2 * half))]
    assert len(side) == N_HEADS
    slack = None
    for hd in range(N_HEADS):
        qa, qr = _head_query(q_nope, q_rope, cs, w_ukt_ref, hd)
        side[hd]()
        qr_m = jnp.where(is_rope, qr, 0.0)
        c = jnp.sqrt(rsum(qa * qa + qr_m * qr_m) * k_norm2) * 1.01 + 1.0
        own = rsum(qa * ckv + qr_m * kr_m)
        worst = jnp.max(c - own, axis=0, keepdims=True)
        slack = worst if slack is None else jnp.maximum(slack, worst)
        qcat_ref[hd, :, 0:LANES] = qa.astype(BF16)
        qcat_ref[hd, :, LANES:QCAT] = jnp.where(is_shift, -c, qr_m).astype(BF16)
    slack_ref[0] = jnp.broadcast_to(slack, slack_ref.shape[1:])
    kcat_ref[:, 0:LANES] = ckv.astype(BF16)
    kcat_ref[:, LANES:QCAT] = jnp.where(is_shift, 1.0, kr_m).astype(BF16)
    vt_ref[0] = ckv.T.astype(BF16)
    ckv_ref[...] = ckv
    kr_ref[...] = kr2.T[:QK_ROPE, :]

    u, z_pool = side_val["u"], side_val["z_pool"]
    g_pool = jnp.concatenate([side_val["g_pool0"], side_val["g_pool1"]], axis=1)
    uext_ref[hist_rows:, :] = u
    pos = i * tm + lax.broadcasted_iota(jnp.int32, (tm, 1), 0)
    win_sum, rcnt = [], []
    for g, w in enumerate(POOL_WINDOWS):
        c0 = g * POOL_GROUP_DIM
        acc = u[:, c0:c0 + POOL_GROUP_DIM]
        for s in range(1, w):
            acc = acc + uext_ref[hist_rows - s:hist_rows - s + tm, c0:c0 + POOL_GROUP_DIM]
        win_sum.append(acc)
        rcnt.append(1.0 / jnp.minimum(w, pos + 1).astype(F32))
    yp_ref[...] = _pool_branch(win_sum, u, rcnt, z_pool, g_pool, w_pool_ref, pscale_ref[...],
                               w_opool_ref[...]).astype(yp_ref.dtype)
    tail = uext_ref[tm:tm + hist_rows, :]
    pstate_ref[...] = tail
    uext_ref[0:hist_rows, :] = tail


def _prompt_attn_kernel(safe_ref, q_ref, k_ref, vt_ref, o_ref, m_sc, l_sc, acc_sc, *, t):
    i = pl.program_id(0)
    l_sc[...] = jnp.zeros(l_sc.shape, F32)
    acc_sc[...] = jnp.zeros(acc_sc.shape, F32)

    def chunks(specs, shifted):
        kcs = [k_ref.at[pl.ds(pl.multiple_of(j * t, t), t), :] for j, _ in specs]
        vts = [vt_ref.at[j] for j, _ in specs]
        units = [(c, hd) for c in range(len(specs)) for hd in range(N_HEADS)]
        if any(masked for _, masked in specs):
            key = lax.broadcasted_iota(jnp.int32, (t, t), 0)
            qry = lax.broadcasted_iota(jnp.int32, (t, t), 1)
            keep = key <= qry
        hb = t // 2
        split = [shifted and specs[c][1] for c, _ in units]

        def score(c, hd, cut):
            if cut:
                return (_dot_nt(kcs[c][0:hb, :], q_ref[hd]),
                        _dot_nt(kcs[c][hb:t, :], q_ref[hd, hb:t, :]))
            return _dot_nt(kcs[c][...], q_ref[hd])

        scores = [score(c, hd, cut) for (c, hd), cut in zip(units, split)]
        for (c, hd), st, cut in zip(units, scores, split):
            if cut:
                p_top = jnp.exp2(jnp.where(keep[:hb], st[0], NEG_INF))
                p_bot = jnp.exp2(jnp.where(keep[:hb, :hb], st[1], NEG_INF))
                l_sc[hd] += jnp.sum(p_top, axis=0, keepdims=True)
                l_sc[hd, :, hb:] += jnp.sum(p_bot, axis=0, keepdims=True)
                acc_sc[hd] += _dot(vts[c][:, 0:hb], p_top.astype(BF16))
                acc_sc[hd, :, hb:] += _dot(vts[c][:, hb:t], p_bot.astype(BF16))
                continue
            if specs[c][1]:
                st = jnp.where(keep, st, NEG_INF)
            if shifted:
                p = jnp.exp2(st)
                l_sc[hd] += jnp.sum(p, axis=0, keepdims=True)
                acc_sc[hd] += _dot(vts[c][...], p.astype(BF16))
            else:
                m_prev = m_sc[hd]
                m_new = jnp.maximum(m_prev, jnp.max(st, axis=0, keepdims=True))
                alpha = jnp.exp2(m_prev - m_new)
                p = jnp.exp2(st - m_new)
                l_sc[hd] = alpha * l_sc[hd] + jnp.sum(p, axis=0, keepdims=True)
                acc_sc[hd] = alpha * acc_sc[hd] + _dot(vts[c][...], p.astype(BF16))
                m_sc[hd] = m_new

    def sweep(shifted, width):
        def body(jw, carry):
            chunks([(jw * width + c, False) for c in range(width)], shifted)
            return carry
        lax.fori_loop(0, i // width, body, 0)
        for rem in range(width):
            @pl.when(i % width == rem)
            def _(rem=rem):
                chunks([(i - rem + c, False) for c in range(rem)] + [(i, True)], shifted)

    safe = safe_ref[i] != 0

    @pl.when(safe)
    def _():
        sweep(True, 3)

    @pl.when(jnp.logical_not(safe))
    def _():
        m_sc[...] = jnp.full(m_sc.shape, NEG_INF, F32)
        sweep(False, 1)

    for hd in range(N_HEADS):
        o = acc_sc[hd] / l_sc[hd]
        o_ref[:, hd * KV_RANK:(hd + 1) * KV_RANK] = o.T.astype(BF16)


def _epilogue_kernel(o_ref, sz_ref, ga_ref, yp_ref, x_ref, w_uv_ref, w_oatt_ref, w_out_ref,
                     g_final_ref, y_ref):
    o = o_ref[...]
    o2 = jnp.concatenate(
        [_dot(o[:, hd * KV_RANK:(hd + 1) * KV_RANK], w_uv_ref[hd]) for hd in range(N_HEADS)], axis=1)
    y_att = _dot((o2 * sz_ref[...].astype(F32)).astype(BF16), w_oatt_ref[...])
    merged = ga_ref[...].astype(F32) * y_att + yp_ref[...].astype(F32)
    xo = x_ref[...] + _dot(merged.astype(BF16), w_out_ref[...])
    y_ref[...] = _rms(xo, g_final_ref[...])


def _sample_proj_kernel(x_ref, cs_ref, hist_ref, g_pre_ref, w_a_ref, w_za_ref, w_up_ref,
                        w_zp_ref, w_ga_ref, w_gp_ref, g_q_ref, w_qn_ref, w_qr_ref, w_ukt_ref,
                        g_kv_ref, w_pool_ref, pscale_ref, w_opool_ref,
                        qa_ref, qr_ref, ckv_ref, kr_ref, sz_ref, ga_ref, yp_ref, pstate_ref,
                        *, past_len):
    h = _rms(x_ref[...], g_pre_ref[...]).astype(BF16)
    za = _dot_nt(h, w_a_ref[...])
    qa, qr, ckv, kr2 = _queries_keys(za, cs_ref[...], g_q_ref[...], w_qn_ref[...], w_qr_ref[...],
                                     w_ukt_ref, g_kv_ref[...])
    for hd in range(N_HEADS):
        qa_ref[:, hd * LANES:(hd + 1) * LANES] = qa[hd]
        qr_ref[:, hd * LANES:(hd + 1) * LANES] = qr[hd]
    ckv_ref[...] = ckv
    kr_ref[...] = kr2[:, :QK_ROPE]

    u, z_pool, g_pool = _gates_and_pool_inputs(h, w_up_ref, w_zp_ref, w_gp_ref, w_za_ref, w_ga_ref,
                                               sz_ref, ga_ref)
    win_sum, rcnt = [], []
    for g, w in enumerate(POOL_WINDOWS):
        c0 = g * POOL_GROUP_DIM
        acc = u[:, c0:c0 + POOL_GROUP_DIM]
        for s in range(1, w):
            acc = acc + hist_ref[POOL_HIST - s, :, c0:c0 + POOL_GROUP_DIM]
        win_sum.append(acc)
        rcnt.append(1.0 / float(min(w, past_len + 1)))
    yp_ref[...] = _pool_branch(win_sum, u, rcnt, z_pool, g_pool, w_pool_ref, pscale_ref[...],
                               w_opool_ref[...]).astype(yp_ref.dtype)
    pstate_ref[0:POOL_HIST - 1] = hist_ref[1:POOL_HIST]
    pstate_ref[POOL_HIST - 1] = u


def _lane_group_reduce(v, op):
    sh = N_HEADS
    while sh < LANES:
        v = op(v, pltpu.roll(v, sh, axis=1))
        sh *= 2
    return v


def _diag_to_col(v):
    sub = lax.broadcasted_iota(jnp.int32, v.shape, 0)
    lane = lax.broadcasted_iota(jnp.int32, v.shape, 1)
    return jnp.sum(jnp.where(sub == lane, v, 0.0), axis=1, keepdims=True)


def _latent_rows(kbuf):
    n_pairs, _, page, width = kbuf.shape
    halves = []
    for half in range(2):
        cols = []
        for jj in range(KEYS_PER_ROW // 2):
            r0 = (half * (KEYS_PER_ROW // 2) + jj) * 8
            for par in range(2):
                cols.append(kbuf[:, par, r0:r0 + 8, :].reshape(n_pairs * 8, width))
        halves.append(jnp.concatenate(cols, axis=1))
    return jnp.concatenate(halves, axis=0).astype(BF16)


def _rotary_rows(rbuf):
    n_pairs = rbuf.shape[0]
    xt = jnp.swapaxes(rbuf[...], 1, 2)
    halves = []
    for half in range(2):
        cols = []
        for jj in range(KEYS_PER_ROW // 2):
            r0 = (half * (KEYS_PER_ROW // 2) + jj) * 8
            cols.append(xt[:, r0:r0 + 8, :].reshape(n_pairs * 8, LANES))
        halves.append(jnp.concatenate(cols, axis=1))
    return jnp.concatenate(halves, axis=0).astype(BF16)


def _attend_one(qat, qrt, knew, rnew, kbuf, rbuf):
    lane_a = lax.broadcasted_iota(jnp.int32, qat.shape, 1) // N_HEADS
    lane_r = lax.broadcasted_iota(jnp.int32, qrt.shape, 1) // N_HEADS
    zero = jnp.zeros((), BF16)
    wk = jnp.concatenate([jnp.where(lane_a == j, qat, zero) for j in range(KEYS_PER_ROW)], axis=0)
    wr = jnp.concatenate([jnp.where(lane_r == j, qrt, zero) for j in range(KEYS_PER_ROW)], axis=0)

    kb = _latent_rows(kbuf)
    rb = _rotary_rows(rbuf)
    hr = kb.shape[0] // 2
    s_lat = [_dot(kb[:hr], wk), _dot(kb[hr:], wk)]
    s_rot = [_dot(rb[:hr], wr), _dot(rb[hr:], wr)]
    s2 = jnp.concatenate([s_lat[0] + s_rot[0], s_lat[1] + s_rot[1]], axis=0)

    knew_b = knew.astype(BF16)
    rnew_b = rnew.astype(BF16)
    s_new = (_dot(jnp.broadcast_to(knew_b, (8, KV_RANK)), qat)
             + _dot(jnp.broadcast_to(rnew_b, (8, QK_ROPE)), qrt))

    m = jnp.max(s2, axis=0, keepdims=True)
    m = _lane_group_reduce(jnp.broadcast_to(m, (8, LANES)), jnp.maximum)
    m = jnp.maximum(m, s_new)
    p2 = jnp.exp2(s2 - m[0:1])
    p_new = jnp.exp2(s_new - m)
    l_lane = _lane_group_reduce(
        jnp.broadcast_to(jnp.sum(p2, axis=0, keepdims=True), (8, LANES)), jnp.add)
    p_new_col = _diag_to_col(p_new)
    l_col = _diag_to_col(l_lane) + p_new_col

    p2t = p2.T.astype(BF16)
    hs = KEYS_PER_ROW // 2
    hp, hk = hs * N_HEADS, hs * KV_RANK
    g_halves = [_dot(p2t[:hp], kb[:, :hk]), _dot(p2t[hp:], kb[:, hk:])]
    o = p_new_col.astype(BF16).astype(F32) * knew_b.astype(F32)
    for g in g_halves:
        for j in range(hs):
            o = o + g[j * N_HEADS:(j + 1) * N_HEADS, j * KV_RANK:(j + 1) * KV_RANK]
    return o / l_col


def _sample_attn_kernel(pt_ref, qat_ref, qrt_ref, knew_ref, rnew_ref, ck_hbm, krt_hbm, o_ref,
                        kbuf, rbuf, sems, *, n_pages, per_step):
    g = pl.program_id(0)
    n_seq = pl.num_programs(0) * per_step

    def start_pair(seq, slot, q):
        for par in range(2):
            pid = pt_ref[seq * n_pages + 2 * q + par]
            pltpu.make_async_copy(ck_hbm.at[pid], kbuf.at[slot, q, par], sems.at[0, slot]).start()
            pltpu.make_async_copy(
                krt_hbm.at[pid], rbuf.at[slot, q, par * QK_ROPE:(par + 1) * QK_ROPE, :],
                sems.at[1, slot]).start()

    def wait_all(slot):
        pltpu.make_async_copy(kbuf.at[slot], kbuf.at[slot], sems.at[0, slot]).wait()
        pltpu.make_async_copy(rbuf.at[slot], rbuf.at[slot], sems.at[1, slot]).wait()

    @pl.when(g == 0)
    def _():
        def body(q, c):
            for slot in range(DECODE_SLOTS):
                start_pair(slot, slot, q)
            return c
        lax.fori_loop(0, n_pages // 2, body, 0)

    for k in range(per_step):
        seq = g * per_step + k
        slot = seq % DECODE_SLOTS
        wait_all(slot)
        o_ref[k] = _attend_one(qat_ref[k], qrt_ref[k], knew_ref[k], rnew_ref[k],
                               kbuf.at[slot], rbuf.at[slot])
        refill = jnp.minimum(seq + DECODE_SLOTS, n_seq - 1)
        for q in range(n_pages // 2):
            start_pair(refill, slot, q)

    @pl.when(g == pl.num_programs(0) - 1)
    def _():
        for slot in range(DECODE_SLOTS):
            wait_all(slot)


def _const_spec(shape):
    zeros = (0,) * len(shape)
    return pl.BlockSpec(shape, lambda *_: zeros)


def _prep_weights(g_pre, w_in, g_q, w_uq, g_kv, w_uk, w_uv, w_o_att, w_pool, pool_scale,
                  w_o_pool, w_out, g_final):
    d_model = w_in.shape[0]
    att_w = N_HEADS * V_DIM
    half = QK_ROPE // 2
    o_kv = Q_RANK
    o_kr = o_kv + KV_RANK
    o_za = o_kr + QK_ROPE
    o_up = o_za + att_w
    o_zp = o_up + POOL_WIDTH
    o_ga = o_zp + POOL_WIDTH
    o_gp = o_ga + d_model
    w_t = w_in.T
    w_kr = w_t[o_kr:o_za]
    w_a = jnp.concatenate([w_t[:o_kr], w_kr, -w_kr[half:], w_kr[:half]], axis=0)
    uq = w_uq.reshape(Q_RANK, N_HEADS, QK_NOPE + QK_ROPE)
    w_qn = uq[:, :, :QK_NOPE].reshape(Q_RANK, N_HEADS * QK_NOPE)
    x1, x2 = uq[:, :, QK_NOPE:QK_NOPE + half], uq[:, :, QK_NOPE + half:]
    w_qr = jnp.concatenate([x1, x2, -x2, x1], axis=2).reshape(Q_RANK, N_HEADS * LANES)
    bf = lambda a: a.astype(BF16)
    row = lambda a: a.reshape(1, -1).astype(F32)
    return dict(
        g_pre=row(g_pre), w_a=bf(w_a), w_za=bf(w_t[o_za:o_up]), w_up=bf(w_t[o_up:o_zp]),
        w_zp=bf(w_t[o_zp:o_ga]), w_ga=bf(w_t[o_ga:o_gp]), w_gp=bf(w_t[o_gp:]),
        g_q=row(g_q), w_qn=bf(w_qn), w_qr=bf(w_qr),
        w_ukt=bf(jnp.transpose(w_uk, (1, 2, 0))),
        g_kv=row(g_kv), w_pool=bf(w_pool), pscale=row(pool_scale), w_opool=bf(w_o_pool),
        w_uv=bf(jnp.transpose(w_uv, (1, 0, 2))),
        w_oatt=bf(w_o_att), w_out=bf(w_out), g_final=row(g_final))


_PROJ_WEIGHTS = ("g_pre", "w_a", "w_za", "w_up", "w_zp", "w_ga", "w_gp", "g_q", "w_qn", "w_qr",
                 "w_ukt", "g_kv", "w_pool", "pscale", "w_opool")


def _cos_sin(pos):
    half = QK_ROPE // 2
    inv = ROPE_BASE ** (-jnp.arange(half, dtype=F32) / half)
    ang = pos.astype(F32)[:, None] * inv[None, :]
    return jnp.cos(ang), jnp.sin(ang)


def _rope_table(pos):
    c, s = _cos_sin(pos)
    return jnp.concatenate([c, c, s, s], axis=1)


ROPE_BLOCK = 128


def _rope_factors(seq):
    ca, sa = _cos_sin(jnp.arange(0, seq, ROPE_BLOCK, dtype=jnp.int32))
    cb, sb = _cos_sin(jnp.arange(ROPE_BLOCK, dtype=jnp.int32))
    rope_a = jnp.stack([jnp.concatenate([ca, ca, sa, sa], axis=1),
                        jnp.concatenate([-sa, -sa, ca, ca], axis=1)], axis=1)
    rope_b = jnp.stack([jnp.concatenate([cb] * 4, axis=1),
                        jnp.concatenate([sb] * 4, axis=1)], axis=0)
    return rope_a, rope_b


def _prompt_proj(x, rope_a, rope_b, hist, wts, tm, t_attn):
    seq, d_model = x.shape
    n = seq // tm
    per_attn = t_attn // tm
    blocks = tm // ROPE_BLOCK
    hist_rows = hist.shape[0]
    ws = [wts[k] for k in _PROJ_WEIGHTS]
    rows = lambda width: pl.BlockSpec((tm, width), lambda i: (i, 0))
    out_shape = (
        jax.ShapeDtypeStruct((N_HEADS, seq, QCAT), BF16),
        jax.ShapeDtypeStruct((seq, QCAT), BF16),
        jax.ShapeDtypeStruct((seq // t_attn, KV_RANK, t_attn), BF16),
        jax.ShapeDtypeStruct((seq, KV_RANK), F32),
        jax.ShapeDtypeStruct((QK_ROPE, seq), F32),
        jax.ShapeDtypeStruct((seq, d_model), BF16),
        jax.ShapeDtypeStruct((seq, d_model), BF16),
        jax.ShapeDtypeStruct((seq, d_model), BF16),
        jax.ShapeDtypeStruct((hist_rows, POOL_WIDTH), F32),
        jax.ShapeDtypeStruct((n, 8, LANES), F32),
    )
    out_specs = (
        pl.BlockSpec((N_HEADS, tm, QCAT), lambda i: (0, i, 0)),
        rows(QCAT),
        pl.BlockSpec((1, KV_RANK, tm), lambda i: (i // per_attn, 0, i % per_attn)),
        rows(KV_RANK), pl.BlockSpec((QK_ROPE, tm), lambda i: (0, i)),
        rows(d_model), rows(d_model), rows(d_model),
        _const_spec((hist_rows, POOL_WIDTH)),
        pl.BlockSpec((1, 8, LANES), lambda i: (i, 0, 0)),
    )
    return pl.pallas_call(
        functools.partial(_prompt_proj_kernel, tm=tm),
        grid=(n,),
        in_specs=[rows(d_model), pl.BlockSpec((blocks, 2, LANES), lambda i: (i, 0, 0)),
                  _const_spec(rope_b.shape), _const_spec(hist.shape)]
        + [_const_spec(w.shape) for w in ws],
        out_specs=out_specs,
        out_shape=out_shape,
        scratch_shapes=[pltpu.VMEM((hist_rows + tm, POOL_WIDTH), F32),
                        pltpu.VMEM((8, LANES), F32)],
        compiler_params=pltpu.CompilerParams(
            dimension_semantics=("arbitrary",), vmem_limit_bytes=VMEM_LIMIT),
    )(x, rope_a, rope_b, hist, *ws)


def _prompt_attn(safe, qcat, kcat, vt, t):
    seq = kcat.shape[0]
    n = seq // t
    grid_spec = pltpu.PrefetchScalarGridSpec(
        num_scalar_prefetch=1,
        grid=(n,),
        in_specs=[pl.BlockSpec((N_HEADS, t, QCAT), lambda i, s: (0, i, 0)),
                  pl.BlockSpec(memory_space=pltpu.VMEM),
                  pl.BlockSpec(memory_space=pltpu.VMEM)],
        out_specs=pl.BlockSpec((t, N_HEADS * KV_RANK), lambda i, s: (i, 0)),
        scratch_shapes=[pltpu.VMEM((N_HEADS, 1, t), F32),
                        pltpu.VMEM((N_HEADS, 1, t), F32),
                        pltpu.VMEM((N_HEADS, KV_RANK, t), F32)],
    )
    return pl.pallas_call(
        functools.partial(_prompt_attn_kernel, t=t),
        grid_spec=grid_spec,
        out_shape=jax.ShapeDtypeStruct((seq, N_HEADS * KV_RANK), BF16),
        compiler_params=pltpu.CompilerParams(
            dimension_semantics=("arbitrary",), vmem_limit_bytes=VMEM_LIMIT),
    )(safe, qcat, kcat, vt)


def _epilogue(o, sz, ga, yp, x, wts, tm):
    n_rows, d_model = x.shape
    rows = pl.BlockSpec((tm, d_model), lambda i: (i, 0))
    ws = [wts[k] for k in ("w_uv", "w_oatt", "w_out", "g_final")]
    return pl.pallas_call(
        _epilogue_kernel,
        grid=(n_rows // tm,),
        in_specs=[rows] * 5 + [_const_spec(w.shape) for w in ws],
        out_specs=rows,
        out_shape=jax.ShapeDtypeStruct((n_rows, d_model), F32),
        compiler_params=pltpu.CompilerParams(
            dimension_semantics=("arbitrary",), vmem_limit_bytes=VMEM_LIMIT),
    )(o, sz, ga, yp, x, *ws)


def _sample_proj(x, cs, hist, wts, past_len):
    b, d_model = x.shape
    ws = [wts[k] for k in _PROJ_WEIGHTS]
    wide = jax.ShapeDtypeStruct((b, d_model), BF16)
    out_shape = (
        jax.ShapeDtypeStruct((b, N_HEADS * LANES), F32),
        jax.ShapeDtypeStruct((b, N_HEADS * LANES), F32),
        jax.ShapeDtypeStruct((b, KV_RANK), F32),
        jax.ShapeDtypeStruct((b, QK_ROPE), F32),
        wide, wide, wide,
        jax.ShapeDtypeStruct(hist.shape, F32),
    )
    args = (x, cs, hist, *ws)
    return pl.pallas_call(
        functools.partial(_sample_proj_kernel, past_len=past_len),
        grid=(1,),
        in_specs=[_const_spec(a.shape) for a in args],
        out_specs=tuple(_const_spec(s.shape) for s in out_shape),
        out_shape=out_shape,
        compiler_params=pltpu.CompilerParams(
            dimension_semantics=("arbitrary",), vmem_limit_bytes=VMEM_LIMIT),
    )(*args)


def _sample_attn(page_table, qat, qrt, knew, rnew, cache_ckv, cache_krope):
    b, n_pages = page_table.shape
    page = cache_ckv.shape[1]
    per_step = 2
    assert b % per_step == 0 and b >= DECODE_SLOTS and n_pages % 2 == 0
    pair = lambda rows, width: pl.BlockSpec((per_step, rows, width), lambda i, pt: (i, 0, 0))
    kbuf = pltpu.VMEM((DECODE_SLOTS, n_pages // 2, 2, page, KV_RANK), F32)
    rbuf = pltpu.VMEM((DECODE_SLOTS, n_pages // 2, 2 * QK_ROPE, page), F32)
    grid_spec = pltpu.PrefetchScalarGridSpec(
        num_scalar_prefetch=1,
        grid=(b // per_step,),
        in_specs=[pair(KV_RANK, LANES), pair(QK_ROPE, LANES), pair(1, KV_RANK), pair(1, QK_ROPE),
                  pl.BlockSpec(memory_space=pl.ANY),
                  pl.BlockSpec(memory_space=pl.ANY)],
        out_specs=pair(N_HEADS, KV_RANK),
        scratch_shapes=[kbuf, rbuf, pltpu.SemaphoreType.DMA((2, DECODE_SLOTS))],
    )
    return pl.pallas_call(
        functools.partial(_sample_attn_kernel, n_pages=n_pages, per_step=per_step),
        grid_spec=grid_spec,
        out_shape=jax.ShapeDtypeStruct((b, N_HEADS, KV_RANK), F32),
        compiler_params=pltpu.CompilerParams(
            dimension_semantics=("arbitrary",), vmem_limit_bytes=VMEM_LIMIT),
    )(page_table.reshape(-1), qat, qrt, knew, rnew, cache_ckv, jnp.swapaxes(cache_krope, 1, 2))


def kernel(x_prompt, x_sample, cache_ckv, cache_krope, state_pool, page_table, g_pre, w_in, g_q,
           w_uq, g_kv, w_uk, w_uv, w_o_att, w_pool, pool_scale, w_o_pool, w_out, g_final):
    depth = g_pre.shape[0]
    batch, seq, d_model = x_prompt.shape
    dec_b, dec_seq, _ = x_sample.shape
    assert depth == 1 and batch == 1 and dec_seq == 1
    past_len = page_table.shape[1] * cache_ckv.shape[2]
    wts = _prep_weights(g_pre[0], w_in[0], g_q[0], w_uq[0], g_kv[0], w_uk[0], w_uv[0], w_o_att[0],
                        w_pool[0], pool_scale[0], w_o_pool[0], w_out[0], g_final)

    tile = 512
    proj_tile = 512
    hist_rows = 16
    xp = x_prompt.reshape(seq, d_model)
    rope_a, rope_b = _rope_factors(seq)
    hist0 = jnp.zeros((hist_rows, POOL_WIDTH), F32)
    qcat, kcat, vt, ckv_p, krt_p, sz, ga, yp, pstate, slack = _prompt_proj(
        xp, rope_a, rope_b, hist0, wts, proj_tile, tile)
    slack = jnp.max(slack[:, 0, 0].reshape(seq // tile, tile // proj_tile), axis=1)
    safe = (slack <= SAFE_SLACK).astype(jnp.int32)
    o_p = _prompt_attn(safe, qcat, kcat, vt, tile)
    y_p = _epilogue(o_p, sz, ga, yp, xp, wts, 2 * tile)

    xs = x_sample.reshape(dec_b, d_model)
    cs_s = _rope_table(jnp.full((dec_b,), past_len, jnp.int32))
    hist_s = jnp.transpose(state_pool[0], (1, 0, 2))
    qa, qr, ckv_s, kr_s, sz_s, ga_s, yp_s, pstate_s = _sample_proj(xs, cs_s, hist_s, wts, past_len)
    qa3 = qa.reshape(dec_b, N_HEADS, LANES)
    qr3 = qr.reshape(dec_b, N_HEADS, LANES)[:, :, :QK_ROPE]
    qat = jnp.tile(jnp.transpose(qa3, (0, 2, 1)), (1, 1, KEYS_PER_ROW)).astype(BF16)
    qrt = jnp.tile(jnp.transpose(qr3, (0, 2, 1)), (1, 1, KEYS_PER_ROW)).astype(BF16)
    o_s = _sample_attn(page_table, qat, qrt, ckv_s.reshape(dec_b, 1, KV_RANK),
                       kr_s.reshape(dec_b, 1, QK_ROPE), cache_ckv[0], cache_krope[0])
    o_s = o_s.reshape(dec_b, N_HEADS * KV_RANK).astype(BF16)
    y_s = _epilogue(o_s, sz_s, ga_s, yp_s, xs, wts, dec_b)

    return (y_p.reshape(batch, seq, d_model),
            y_s.reshape(dec_b, 1, d_model),
            ckv_p.reshape(1, batch, seq, KV_RANK),
            krt_p.T.reshape(1, batch, seq, QK_ROPE),
            pstate[hist_rows - POOL_HIST:].reshape(1, batch, POOL_HIST, POOL_WIDTH),
            ckv_s.reshape(1, dec_b, 1, KV_RANK),
            kr_s.reshape(1, dec_b, 1, QK_ROPE),
            jnp.transpose(pstate_s, (1, 0, 2)).reshape(1, dec_b, POOL_HIST, POOL_WIDTH))
```
